```python
import jax, jax.numpy as jnp
from jax import lax
import numpy as np

D_MODEL = 4096
BATCH = 2
SEQ = 4096
DEPTH = 2

HEAD_DIM = 128
D_MIX = D_MODEL
LRU_WIDTH = D_MIX // 4
LRU_BLOCKS = LRU_WIDTH // HEAD_DIM
LRU_CONV = 4
LRU_C = 8.0
HGRN_WIDTH = D_MIX // 4
HGRN_HEADS = HGRN_WIDTH // HEAD_DIM
HGRN_CHUNK = 64
FOX_WIDTH = D_MIX // 2
FOX_HEADS = FOX_WIDTH // HEAD_DIM
FOX_BLOCK = 128
D_IN = 2 * LRU_WIDTH + 4 * HGRN_WIDTH + 3 * FOX_WIDTH + FOX_HEADS
D_FF = ((8 * D_MODEL // 3 + 255) // 256) * 256
FFN_CONV = 3
EPS = 1e-6

kernel_name = "hybrid_rglru_hgrn2_fox_parallel_heads"

F32 = jnp.float32


def _split_points():
    sizes = (LRU_WIDTH, LRU_WIDTH,
             HGRN_WIDTH, HGRN_WIDTH, HGRN_WIDTH, HGRN_WIDTH,
             FOX_WIDTH, FOX_WIDTH, FOX_WIDTH, FOX_HEADS)
    return [int(v) for v in np.cumsum(sizes)[:-1]]


def rms_norm(x, gain):
    xf = x.astype(F32)
    var = jnp.mean(xf * xf, axis=-1, keepdims=True)
    return (xf * lax.rsqrt(var + EPS) * gain.astype(F32)).astype(x.dtype)


def head_rms_norm(t, gain):
    b, s, w = t.shape
    th = t.astype(F32).reshape(b, s, w // HEAD_DIM, HEAD_DIM)
    var = jnp.mean(th * th, axis=-1, keepdims=True)
    return (th * lax.rsqrt(var + EPS)).reshape(b, s, w) * gain.astype(F32)


def causal_dwconv(x, w, bias):
    width, ch = w.shape
    y = lax.conv_general_dilated(
        x, w[:, None, :].astype(x.dtype), window_strides=(1,), padding=[(width - 1, 0)],
        dimension_numbers=("NWC", "WIO", "NWC"), feature_group_count=ch)
    return y + bias.astype(x.dtype)


def rg_lru(xc, w_a, b_a, w_x, b_x, lam):
    b, s, _ = xc.shape
    xh = xc.reshape(b, s, LRU_BLOCKS, HEAD_DIM)
    r = jax.nn.sigmoid(jnp.einsum("bsnd,nde->bsne", xh, w_a).reshape(b, s, LRU_WIDTH).astype(F32) + b_a.astype(F32))
    i = jax.nn.sigmoid(jnp.einsum("bsnd,nde->bsne", xh, w_x).reshape(b, s, LRU_WIDTH).astype(F32) + b_x.astype(F32))
    log_a = -LRU_C * r * jax.nn.softplus(-lam.astype(F32))
    a = jnp.exp(log_a)
    u = jnp.sqrt(-jnp.expm1(2.0 * log_a)) * (i * xc.astype(F32))

    def combine(left, right):
        a1, h1 = left
        a2, h2 = right
        return a1 * a2, a2 * h1 + h2

    _, h = lax.associative_scan(combine, (a, u), axis=1)
    return h


def hgrn2_chunkwise(q, log_f, k, v):
    b, s, h, dk = q.shape
    dv = v.shape[-1]
    n = s // HGRN_CHUNK

    def to_chunks(t):
        return t.reshape(b, n, HGRN_CHUNK, h, t.shape[-1]).transpose(1, 0, 3, 2, 4)

    causal = jnp.tril(jnp.ones((HGRN_CHUNK, HGRN_CHUNK), dtype=bool))

    def step(state, inp):
        qc, gc, kc, vc = inp
        cum = jnp.cumsum(gc, axis=2)
        o_inter = jnp.einsum("bhtk,bhkv->bhtv", qc * jnp.exp(cum), state)
        diff = cum[:, :, :, None, :] - cum[:, :, None, :, :]
        decay = jnp.exp(jnp.where(causal[:, :, None], diff, -jnp.inf))
        scores = jnp.einsum("bhtk,bhsk,bhtsk->bhts", qc, kc, decay)
        o_intra = jnp.einsum("bhts,bhsv->bhtv", scores, vc)
        last = cum[:, :, -1:, :]
        new_state = (jnp.exp(last[:, :, 0, :])[..., None] * state
                     + jnp.einsum("bhsk,bhsv->bhkv", kc * jnp.exp(last - cum), vc))
        return new_state, o_inter + o_intra

    state0 = jnp.zeros((b, h, dk, dv), F32)
    _, o = lax.scan(step, state0, (to_chunks(q), to_chunks(log_f), to_chunks(k), to_chunks(v)))
    return o.transpose(1, 0, 3, 2, 4).reshape(b, s, h * dv)


def forgetting_attention(q, k, v, log_f):
    b, s, h, d = q.shape
    cum = jnp.cumsum(log_f, axis=1).transpose(0, 2, 1)
    scale = d ** -0.5
    outs = []
    for blk in range(s // FOX_BLOCK):
        q0, q1 = blk * FOX_BLOCK, (blk + 1) * FOX_BLOCK
        logits = jnp.einsum("bqhd,bkhd->bhqk", q[:, q0:q1], k[:, :q1]).astype(F32) * scale
        logits = logits + cum[:, :, q0:q1, None] - cum[:, :, None, :q1]
        mask = (q0 + jnp.arange(FOX_BLOCK))[:, None] >= jnp.arange(q1)[None, :]
        probs = jax.nn.softmax(jnp.where(mask, logits, -jnp.inf), axis=-1)
        outs.append(jnp.einsum("bhqk,bkhd->bqhd", probs.astype(v.dtype), v[:, :q1]))
    return jnp.concatenate(outs, axis=1).reshape(b, s, h * d)


def setup_inputs(seed: int = 0) -> dict:
    key = jax.random.key(seed)
    ks = jax.random.split(key, 24)
    nrm = jax.random.normal
    p_a0 = jax.random.uniform(ks[9], (DEPTH, LRU_WIDTH), F32, 0.9, 0.999) ** (1.0 / LRU_C)
    return {
        "x": nrm(ks[0], (BATCH, SEQ, D_MODEL), F32),
        "ln1_w": 1.0 + 0.02 * nrm(ks[1], (DEPTH, D_MODEL), F32),
        "w_in": nrm(ks[2], (DEPTH, D_MODEL, D_IN), F32) * D_MODEL ** -0.5,
        "lru_conv_w": nrm(ks[3], (DEPTH, LRU_CONV, LRU_WIDTH), F32) * LRU_CONV ** -0.5,
        "lru_conv_b": 0.01 * nrm(ks[4], (DEPTH, LRU_WIDTH), F32),
        "lru_gate_a_w": nrm(ks[5], (DEPTH, LRU_BLOCKS, HEAD_DIM, HEAD_DIM), F32) * HEAD_DIM ** -0.5,
        "lru_gate_a_b": 0.01 * nrm(ks[6], (DEPTH, LRU_WIDTH), F32),
        "lru_gate_x_w": nrm(ks[7], (DEPTH, LRU_BLOCKS, HEAD_DIM, HEAD_DIM), F32) * HEAD_DIM ** -0.5,
        "lru_gate_x_b": 0.01 * nrm(ks[8], (DEPTH, LRU_WIDTH), F32),
        "lru_lambda": jnp.log(p_a0) - jnp.log1p(-p_a0),
        "hgrn_lb_logits": 0.1 * nrm(ks[10], (DEPTH, HGRN_WIDTH), F32),
        "fox_f_bias": jax.random.uniform(ks[11], (DEPTH, FOX_HEADS), F32, 1.0, 6.0),
        "mix_norm_w": 1.0 + 0.02 * nrm(ks[12], (DEPTH, D_MIX), F32),
        "w_out": nrm(ks[13], (DEPTH, D_MIX, D_MODEL), F32) * D_MIX ** -0.5,
        "ln2_w": 1.0 + 0.02 * nrm(ks[14], (DEPTH, D_MODEL), F32),
        "ffn_w_up": nrm(ks[15], (DEPTH, D_MODEL, 2 * D_FF), F32) * D_MODEL ** -0.5,
        "ffn_conv_w": nrm(ks[16], (DEPTH, FFN_CONV, 2 * D_FF), F32) * FFN_CONV ** -0.5,
        "ffn_conv_b": 0.01 * nrm(ks[17], (DEPTH, 2 * D_FF), F32),
        "ffn_w_down": nrm(ks[18], (DEPTH, D_FF, D_MODEL), F32) * D_FF ** -0.5,
        "final_norm_w": 1.0 + 0.02 * nrm(ks[19], (D_MODEL,), F32),
    }


def reference(x, ln1_w, w_in, lru_conv_w, lru_conv_b, lru_gate_a_w, lru_gate_a_b, lru_gate_x_w,
              lru_gate_x_b, lru_lambda, hgrn_lb_logits, fox_f_bias, mix_norm_w, w_out, ln2_w,
              ffn_w_up, ffn_conv_w, ffn_conv_b, ffn_w_down, final_norm_w):
    dt = x.dtype
    b, s, _ = x.shape
    splits = _split_points()
    lb_cum = jnp.cumsum(jax.nn.softmax(hgrn_lb_logits.astype(F32), axis=0), axis=0)
    lower_bounds = lb_cum - lb_cum[0:1]
    g_lru_end, g_hg_end = LRU_WIDTH, LRU_WIDTH + HGRN_WIDTH

    for l in range(DEPTH):
        hn = rms_norm(x, ln1_w[l])
        proj = jnp.einsum("bsd,de->bse", hn, w_in[l])
        (lru_x, lru_y, hg_q, hg_f, hg_i, hg_g,
         fx_q, fx_k, fx_v, fx_f) = jnp.split(proj, splits, axis=-1)
        gains = mix_norm_w[l]

        xc = causal_dwconv(lru_x, lru_conv_w[l], lru_conv_b[l])
        h_lru = rg_lru(xc, lru_gate_a_w[l], lru_gate_a_b[l], lru_gate_x_w[l], lru_gate_x_b[l], lru_lambda[l])
        out_lru = head_rms_norm(h_lru, gains[:g_lru_end]) * jax.nn.gelu(lru_y.astype(F32))

        lb = lower_bounds[l]
        z = hg_f.astype(F32)
        log_f_hg = jnp.logaddexp(jnp.log(lb), jnp.log1p(-lb) + jax.nn.log_sigmoid(z))
        k_hg = (1.0 - lb) * jax.nn.sigmoid(-z)
        q_hg = jax.nn.silu(hg_q.astype(F32))
        shp = (b, s, HGRN_HEADS, HEAD_DIM)
        o_hg = hgrn2_chunkwise(q_hg.reshape(shp), log_f_hg.reshape(shp), k_hg.reshape(shp),
                               hg_i.astype(F32).reshape(shp))
        out_hg = head_rms_norm(o_hg, gains[g_lru_end:g_hg_end]) * jax.nn.sigmoid(hg_g.astype(F32))

        log_f_fox = jax.nn.log_sigmoid(fx_f.astype(F32) + fox_f_bias[l].astype(F32))
        fshp = (b, s, FOX_HEADS, HEAD_DIM)
        o_fox = forgetting_attention(fx_q.reshape(fshp), fx_k.reshape(fshp), fx_v.reshape(fshp), log_f_fox)
        out_fox = head_rms_norm(o_fox, gains[g_hg_end:])

        mixed = jnp.concatenate([out_lru.astype(dt), out_hg.astype(dt), out_fox.astype(dt)], axis=-1)
        x = x + jnp.einsum("bse,ed->bsd", mixed, w_out[l])

        hn2 = rms_norm(x, ln2_w[l])
        up = causal_dwconv(jnp.einsum("bsd,df->bsf", hn2, ffn_w_up[l]), ffn_conv_w[l], ffn_conv_b[l])
        gate, val = jnp.split(up, 2, axis=-1)
        x = x + jnp.einsum("bsf,fd->bsd", jax.nn.silu(gate) * val, ffn_w_down[l])

    return rms_norm(x, final_norm_w)
```

```python
import functools

import jax
import jax.numpy as jnp
from jax import lax
from jax.experimental import pallas as pl
from jax.experimental.pallas import tpu as pltpu

F32 = jnp.float32
BF16 = jnp.bfloat16

HEAD_DIM = 128
LRU_C = 8.0
EPS = 1e-6
NEG_BIG = -1e30

SUBLANES = 8
LANES = 128
VMEM_CAP_BYTES = 64 * 1024 * 1024
VMEM_MARGIN_BYTES = 6 * 1024 * 1024

NT_DIMS = (((1,), (1,)), ((), ()))


def _vmem_limit(estimate_bytes):
    return int(min(max(2 * estimate_bytes, 16 * 1024 * 1024), VMEM_CAP_BYTES - VMEM_MARGIN_BYTES))


def _params(semantics, vmem_estimate):
    return pltpu.CompilerParams(dimension_semantics=semantics, vmem_limit_bytes=_vmem_limit(vmem_estimate))


def _pick(n, candidates):
    for c in candidates:
        if n % c == 0:
            return c
    raise ValueError(f"no tile in {candidates} divides {n}")


def _row_iota(shape):
    return lax.broadcasted_iota(jnp.int32, shape, 0)


def _shift_rows(x, shift, fill):
    rolled = pltpu.roll(x, shift, 0)
    return jnp.where(_row_iota(x.shape) >= shift, rolled, fill)


def _shift_rows_carry(x, shift, tail):
    y = pltpu.roll(x, shift, 0)
    rows = _row_iota(x.shape)
    nt = tail.shape[0]
    for r in range(shift):
        y = jnp.where(rows == r, tail[nt - shift + r:nt - shift + r + 1, :], y)
    return y


def _sigmoid(x):
    return 1.0 / (1.0 + jnp.exp(-x))


def _log_sigmoid(x):
    return jnp.minimum(x, 0.0) - jnp.log1p(jnp.exp(-jnp.abs(x)))


def _softplus(x):
    return jnp.maximum(x, 0.0) + jnp.log1p(jnp.exp(-jnp.abs(x)))


def _head_rms(o, gain):
    var = jnp.mean(o * o, axis=-1, keepdims=True)
    return o * lax.rsqrt(var + EPS) * gain


def _rmsnorm_kernel(x_ref, g_ref, o_ref):
    x = x_ref[...]
    var = jnp.mean(x * x, axis=-1, keepdims=True)
    o_ref[...] = (x * lax.rsqrt(var + EPS) * g_ref[...]).astype(o_ref.dtype)


def _rmsnorm(x, gains, layer, out_dtype):
    n, d = x.shape
    rows = _pick(n, (256, 128, 64, 8))
    est = 2 * rows * d * (4 + jnp.dtype(out_dtype).itemsize)
    return pl.pallas_call(
        _rmsnorm_kernel,
        grid=(n // rows,),
        in_specs=[pl.BlockSpec((rows, d), lambda i: (i, 0)),
                  pl.BlockSpec((None, 1, d), lambda i: (layer, 0, 0))],
        out_specs=pl.BlockSpec((rows, d), lambda i: (i, 0)),
        out_shape=jax.ShapeDtypeStruct((n, d), out_dtype),
        compiler_params=_params(("arbitrary",), est),
        name="rmsnorm",
    )(x, gains)


def _cast_weight(w_ref, wbf_ref, chunk):
    def body(c, carry):
        r = pl.multiple_of(c * chunk, chunk)
        wbf_ref[pl.ds(r, chunk), :] = w_ref[pl.ds(r, chunk), :].astype(BF16)
        return carry
    lax.fori_loop(0, w_ref.shape[0] // chunk, body, 0)


def _mm_kernel(*refs, a_segments, has_res, cast_chunk):
    n_a = len(a_segments)
    a_refs = refs[:n_a]
    w_ref = refs[n_a]
    res_ref = refs[n_a + 1] if has_res else None
    o_ref = refs[n_a + 1 + has_res]
    wbf_ref = refs[n_a + 2 + has_res]

    @pl.when(pl.program_id(1) == 0)
    def _():
        _cast_weight(w_ref, wbf_ref, cast_chunk)

    acc = None
    for a_ref, (off, width) in zip(a_refs, a_segments):
        part = jnp.dot(a_ref[...], wbf_ref[off:off + width, :], preferred_element_type=F32)
        acc = part if acc is None else acc + part
    if has_res:
        acc = acc + res_ref[...]
    o_ref[...] = acc.astype(o_ref.dtype)


def _matmul(a_list, w, *, w_lead, k_block, k_size, n_cols, col_block0=0, bn, bm, res=None,
            out_dtype=F32, name):
    n = a_list[0].shape[0]
    segs, off = [], 0
    for a in a_list:
        width = k_size if len(a_list) == 1 else a.shape[1]
        segs.append((off, width))
        off += width
    assert off == k_size
    lead = tuple(w_lead)
    w_block = (None,) * len(lead) + (k_size, bn)
    in_specs = [pl.BlockSpec((bm, width), lambda j, i, kb=(k_block if len(a_list) == 1 else 0): (i, kb))
                for (_, width) in segs]
    in_specs.append(pl.BlockSpec(w_block, lambda j, i: lead + (k_block, j + col_block0)))
    args = list(a_list) + [w]
    if res is not None:
        in_specs.append(pl.BlockSpec((bm, bn), lambda j, i: (i, j)))
        args.append(res)
    out_bytes = jnp.dtype(out_dtype).itemsize
    est = (2 * k_size * bn * 4 + k_size * bn * 2 + 2 * bm * k_size * 2 + 2 * bm * bn * out_bytes
           + (2 * bm * bn * 4 if res is not None else 0) + bm * bn * 4)
    kern = functools.partial(_mm_kernel, a_segments=tuple(segs), has_res=res is not None,
                             cast_chunk=_pick(k_size, (256, 128, 64, 8)))
    return pl.pallas_call(
        kern,
        grid=(n_cols // bn, n // bm),
        in_specs=in_specs,
        out_specs=pl.BlockSpec((bm, bn), lambda j, i: (i, j)),
        out_shape=jax.ShapeDtypeStruct((n, n_cols), out_dtype),
        scratch_shapes=[pltpu.VMEM((k_size, bn), BF16)],
        compiler_params=pltpu.CompilerParams(dimension_semantics=("arbitrary", "arbitrary"),
                                             vmem_limit_bytes=min(est + VMEM_MARGIN_BYTES,
                                                                  VMEM_CAP_BYTES - VMEM_MARGIN_BYTES // 2)),
        name=name,
    )(*args)


def _lru_kernel(x_ref, y_ref, cw_ref, cb_ref, wa_ref, ba_ref, wx_ref, bx_ref, lam_ref, gain_ref,
                o_ref, xtail_ref, hcar_ref):
    t, width = x_ref.shape
    nblk = width // HEAD_DIM

    @pl.when(pl.program_id(1) == 0)
    def _():
        xtail_ref[...] = jnp.zeros_like(xtail_ref)
        hcar_ref[...] = jnp.zeros_like(hcar_ref)

    x = x_ref[...]
    tail = xtail_ref[...]
    cw = cw_ref[...]
    kw = cw.shape[0]
    xc = cw[kw - 1:kw, :] * x + cb_ref[...]
    for s in range(1, kw):
        xc = xc + cw[kw - 1 - s:kw - s, :] * _shift_rows_carry(x, s, tail)
    xtail_ref[...] = x[t - SUBLANES:t, :]

    xcb = xc.astype(BF16)
    ra, rx = [], []
    for n in range(nblk):
        xs = xcb[:, n * HEAD_DIM:(n + 1) * HEAD_DIM]
        ra.append(jnp.dot(xs, wa_ref[n].astype(BF16), preferred_element_type=F32))
        rx.append(jnp.dot(xs, wx_ref[n].astype(BF16), preferred_element_type=F32))
    r = _sigmoid(jnp.concatenate(ra, axis=1) + ba_ref[...])
    gi = _sigmoid(jnp.concatenate(rx, axis=1) + bx_ref[...])
    log_a = (-LRU_C) * r * _softplus(-lam_ref[...])
    a = jnp.exp(log_a)
    u = jnp.sqrt(-jnp.tanh(log_a) * (a * a + 1.0)) * (gi * xc)

    sh = 1
    while sh < t:
        u = u + a * _shift_rows(u, sh, 0.0)
        a = a * _shift_rows(a, sh, 1.0)
        sh *= 2
    h = u + a * hcar_ref[0:1, :]
    hcar_ref[...] = jnp.broadcast_to(h[t - 1:t, :], hcar_ref.shape)

    gate = jax.nn.gelu(y_ref[...])
    gain = gain_ref[...]
    for n in range(nblk):
        sl = slice(n * HEAD_DIM, (n + 1) * HEAD_DIM)
        o_ref[:, sl] = (_head_rms(h[:, sl], gain[:, sl]) * gate[:, sl]).astype(o_ref.dtype)


def _lru_group(proj, p, layer, batch, seq):
    width = p["lru_conv_w"].shape[-1]
    nblk = width // HEAD_DIM
    t = _pick(seq, (256, 128, 64, 8))
    nc = seq // t
    kw = p["lru_conv_w"].shape[1]
    row = lambda b, c: (b * nc + c, 0)
    vec = lambda b, c: (layer, 0, 0)
    est = 2 * 2 * t * width * 4 + 2 * t * width * 2 + 16 * t * width * 4
    return pl.pallas_call(
        _lru_kernel,
        grid=(batch, nc),
        in_specs=[pl.BlockSpec((t, width), row),
                  pl.BlockSpec((t, width), lambda b, c: (b * nc + c, 1)),
                  pl.BlockSpec((None, kw, width), vec),
                  pl.BlockSpec((None, 1, width), vec),
                  pl.BlockSpec((None, nblk, HEAD_DIM, HEAD_DIM), lambda b, c: (layer, 0, 0, 0)),
                  pl.BlockSpec((None, 1, width), vec),
                  pl.BlockSpec((None, nblk, HEAD_DIM, HEAD_DIM), lambda b, c: (layer, 0, 0, 0)),
                  pl.BlockSpec((None, 1, width), vec),
                  pl.BlockSpec((None, 1, width), vec),
                  pl.BlockSpec((None, 1, width), vec)],
        out_specs=pl.BlockSpec((t, width), row),
        out_shape=jax.ShapeDtypeStruct((batch * seq, width), BF16),
        scratch_shapes=[pltpu.VMEM((SUBLANES, width), F32), pltpu.VMEM((SUBLANES, width), F32)],
        compiler_params=_params(("arbitrary", "arbitrary"), est),
        name="rg_lru",
    )(proj, proj, p["lru_conv_w"], p["lru_conv_b"], p["lru_gate_a_w"], p["lru_gate_a_b"],
      p["lru_gate_x_w"], p["lru_gate_x_b"], p["lru_lambda"], p["mix_norm_w"])


def _hgrn_kernel(q_ref, f_ref, i_ref, g_ref, lbl_ref, gain_ref, o_ref, state_ref, *, layer):
    t, width = q_ref.shape
    nheads = width // HEAD_DIM

    @pl.when(pl.program_id(1) == 0)
    def _():
        state_ref[...] = jnp.zeros_like(state_ref)

    lbl = lbl_ref[...]
    pe = jnp.exp(lbl - jnp.max(lbl, axis=0, keepdims=True))
    psm = pe / jnp.sum(pe, axis=0, keepdims=True)
    lb = jnp.zeros((1, width), F32)
    for d in range(1, layer + 1):
        lb = lb + psm[d:d + 1, :]

    z = f_ref[...]
    la = jnp.log(lb)
    lbz = jnp.log1p(-lb) + _log_sigmoid(z)
    mx = jnp.maximum(la, lbz)
    log_f = mx + jnp.log(jnp.exp(la - mx) + jnp.exp(lbz - mx))
    k_all = (1.0 - lb) * _sigmoid(-z)
    qz = q_ref[...]
    q_all = qz * _sigmoid(qz)
    v_all = i_ref[...]
    gate = _sigmoid(g_ref[...])
    gain = gain_ref[...]

    rows = _row_iota((t, 1))
    rr = _row_iota((t, t))
    cc = lax.broadcasted_iota(jnp.int32, (t, t), 1)

    for h in range(nheads):
        sl = slice(h * HEAD_DIM, (h + 1) * HEAD_DIM)
        q, k, v = q_all[:, sl], k_all[:, sl], v_all[:, sl]
        cum = log_f[:, sl]
        sh = 1
        while sh < t:
            cum = cum + _shift_rows(cum, sh, 0.0)
            sh *= 2
        last = cum[t - 1:t, :]
        st = state_ref[h]
        o = lax.dot_general((q * jnp.exp(cum)).astype(BF16), st.astype(BF16), NT_DIMS,
                            preferred_element_type=F32)

        s_mat = jnp.zeros((t, t), F32)
        m = SUBLANES
        while 2 * m <= t:
            cref = jnp.concatenate(
                [jnp.broadcast_to(cum[b * 2 * m + m - 1:b * 2 * m + m, :], (2 * m, HEAD_DIM))
                 for b in range(t // (2 * m))], axis=0)
            e = jnp.exp(-jnp.abs(cum - cref))
            qrole = (rows % (2 * m)) >= m
            qp = jnp.where(qrole, q * e, 0.0).astype(BF16)
            kp = jnp.where(qrole, 0.0, k * e).astype(BF16)
            s_l = lax.dot_general(qp, kp, NT_DIMS, preferred_element_type=F32)
            s_mat = s_mat + jnp.where((rr // (2 * m)) == (cc // (2 * m)), s_l, 0.0)
            m *= 2
        for d in range(SUBLANES):
            if d == 0:
                prod = q * k
            else:
                dec = jnp.exp(jnp.minimum(cum - pltpu.roll(cum, d, 0), 0.0))
                prod = q * pltpu.roll(k, d, 0) * dec
            c = jnp.sum(prod, axis=-1, keepdims=True)
            s_mat = s_mat + jnp.where((cc == rr - d) & ((rr % SUBLANES) >= d), c, 0.0)
        o = o + jnp.dot(s_mat.astype(BF16), v.astype(BF16), preferred_element_type=F32)

        kd = (k * jnp.exp(last - cum)).astype(BF16)
        state_ref[h] = st * jnp.exp(last) + jnp.dot(v.T.astype(BF16), kd, preferred_element_type=F32)

        o_ref[:, sl] = (_head_rms(o, gain[:, sl]) * gate[:, sl]).astype(o_ref.dtype)


def _hgrn_group(proj, p, layer, batch, seq, col0):
    depth, width = p["hgrn_lb_logits"].shape
    nheads = width // HEAD_DIM
    t = _pick(seq, (64, 32, 16))
    nc = seq // t
    blk = lambda k: pl.BlockSpec((t, width), lambda b, c, k=k: (b * nc + c, col0 + k))
    gcol = p["lru_conv_w"].shape[-1] // width
    est = 2 * 4 * t * width * 4 + 2 * t * width * 2 + nheads * HEAD_DIM * HEAD_DIM * 4 + 24 * t * width * 4
    return pl.pallas_call(
        functools.partial(_hgrn_kernel, layer=layer),
        grid=(batch, nc),
        in_specs=[blk(0), blk(1), blk(2), blk(3),
                  pl.BlockSpec((depth, width), lambda b, c: (0, 0)),
                  pl.BlockSpec((None, 1, width), lambda b, c: (layer, 0, gcol))],
        out_specs=pl.BlockSpec((t, width), lambda b, c: (b * nc + c, 0)),
        out_shape=jax.ShapeDtypeStruct((batch * seq, width), BF16),
        scratch_shapes=[pltpu.VMEM((nheads, HEAD_DIM, HEAD_DIM), F32)],
        compiler_params=_params(("arbitrary", "arbitrary"), est),
        name="hgrn2",
    )(proj, proj, proj, proj, p["hgrn_lb_logits"], p["mix_norm_w"])


def _fox_cum_kernel(f_ref, b_ref, cum_ref, cumt_ref):
    s = f_ref.shape[0]
    cum = _log_sigmoid(f_ref[...] + b_ref[...])
    sh = 1
    while sh < s:
        cum = cum + _shift_rows(cum, sh, 0.0)
        sh *= 2
    cum_ref[...] = cum
    cumt_ref[...] = cum.T


def _fox_cum(flog, bias_row, batch, seq):
    est = 2 * 3 * seq * LANES * 4 + 8 * seq * LANES * 4
    return pl.pallas_call(
        _fox_cum_kernel,
        grid=(batch,),
        in_specs=[pl.BlockSpec((seq, LANES), lambda b: (b, 0)),
                  pl.BlockSpec((1, LANES), lambda b: (0, 0))],
        out_specs=[pl.BlockSpec((seq, LANES), lambda b: (b, 0)),
                   pl.BlockSpec((None, LANES, seq), lambda b: (b, 0, 0))],
        out_shape=[jax.ShapeDtypeStruct((batch * seq, LANES), F32),
                   jax.ShapeDtypeStruct((batch, LANES, seq), F32)],
        compiler_params=_params(("arbitrary",), est),
        name="fox_cum",
    )(flog, bias_row)


def _fox_kernel(q_ref, k_ref, v_ref, cq_ref, ck_ref, gain_ref, o_ref, kbf_ref, vbf_ref, m_ref, l_ref, acc_ref):
    tq = q_ref.shape[0]
    seq = k_ref.shape[0]
    head = pl.program_id(1)
    qi = pl.program_id(2)

    @pl.when(qi == 0)
    def _():
        _cast_weight(k_ref, kbf_ref, tq)
        _cast_weight(v_ref, vbf_ref, tq)

    q = (q_ref[...] * (HEAD_DIM ** -0.5)).astype(BF16)
    lane = lax.broadcasted_iota(jnp.int32, (1, LANES), 1)
    cq = jnp.sum(jnp.where(lane == head, cq_ref[...], 0.0), axis=-1, keepdims=True)

    m_ref[...] = jnp.full_like(m_ref, NEG_BIG)
    l_ref[...] = jnp.zeros_like(l_ref)
    acc_ref[...] = jnp.zeros_like(acc_ref)

    def step(ki, masked):
        r0 = pl.multiple_of(ki * tq, tq)
        ks = kbf_ref[pl.ds(r0, tq), :]
        vs = vbf_ref[pl.ds(r0, tq), :]
        s = lax.dot_general(q, ks, NT_DIMS, preferred_element_type=F32)
        s = s + (cq - ck_ref[pl.ds(ki, 1), :])
        if masked:
            s = jnp.where(_row_iota((tq, tq)) >= lax.broadcasted_iota(jnp.int32, (tq, tq), 1), s, NEG_BIG)
        m_old = m_ref[...]
        m_new = jnp.maximum(m_old, jnp.max(s, axis=-1, keepdims=True))
        alpha = jnp.exp(m_old - m_new)
        pr = jnp.exp(s - m_new)
        l_ref[...] = alpha * l_ref[...] + jnp.sum(pr, axis=-1, keepdims=True)
        acc_ref[...] = alpha * acc_ref[...] + jnp.dot(pr.astype(BF16), vs, preferred_element_type=F32)
        m_ref[...] = m_new

    def body(ki, carry):
        step(ki, False)
        return carry

    lax.fori_loop(0, qi, body, 0)
    step(qi, True)

    o = acc_ref[...] / l_ref[...]
    o_ref[...] = _head_rms(o, gain_ref[...]).astype(o_ref.dtype)
    del seq


def _fox_group(proj, cum, cumt, p, layer, batch, seq, qcol0, gcol0):
    nheads = p["fox_f_bias"].shape[-1]
    tq = _pick(seq, (512, 256, 128))
    nq = seq // tq
    cumt4 = cumt.reshape(batch, LANES, nq, tq)
    est = (2 * tq * HEAD_DIM * 4 + 2 * 2 * seq * HEAD_DIM * 4 + 2 * seq * HEAD_DIM * 2 + 2 * tq * LANES * 4
           + 2 * nq * tq * 4 + 2 * tq * HEAD_DIM * 2 + 3 * tq * LANES * 4 + 6 * tq * tq * 4)
    return pl.pallas_call(
        _fox_kernel,
        grid=(batch, nheads, nq),
        in_specs=[pl.BlockSpec((tq, HEAD_DIM), lambda b, h, i: (b * nq + i, qcol0 + h)),
                  pl.BlockSpec((seq, HEAD_DIM), lambda b, h, i: (b, qcol0 + nheads + h)),
                  pl.BlockSpec((seq, HEAD_DIM), lambda b, h, i: (b, qcol0 + 2 * nheads + h)),
                  pl.BlockSpec((tq, LANES), lambda b, h, i: (b * nq + i, 0)),
                  pl.BlockSpec((None, None, nq, tq), lambda b, h, i: (b, h, 0, 0)),
                  pl.BlockSpec((None, 1, HEAD_DIM), lambda b, h, i: (layer, 0, gcol0 + h))],
        out_specs=pl.BlockSpec((tq, HEAD_DIM), lambda b, h, i: (b * nq + i, h)),
        out_shape=jax.ShapeDtypeStruct((batch * seq, nheads * HEAD_DIM), BF16),
        scratch_shapes=[pltpu.VMEM((seq, HEAD_DIM), BF16), pltpu.VMEM((seq, HEAD_DIM), BF16),
                        pltpu.VMEM((tq, 1), F32), pltpu.VMEM((tq, 1), F32), pltpu.VMEM((tq, HEAD_DIM), F32)],
        compiler_params=_params(("arbitrary", "arbitrary", "arbitrary"), est),
        name="fox_attention",
    )(proj, proj, proj, cum, cumt4, p["mix_norm_w"])


def _ffn_up_kernel(a_ref, wg_ref, wv_ref, cwg_ref, cwv_ref, cbg_ref, cbv_ref, o_ref,
                   wgbf_ref, wvbf_ref, tailg_ref, tailv_ref, *, tiles_per_seq, cast_chunk):
    i = pl.program_id(1)
    bm = a_ref.shape[0]

    @pl.when(i == 0)
    def _():
        _cast_weight(wg_ref, wgbf_ref, cast_chunk)
        _cast_weight(wv_ref, wvbf_ref, cast_chunk)

    @pl.when(i % tiles_per_seq == 0)
    def _():
        tailg_ref[...] = jnp.zeros_like(tailg_ref)
        tailv_ref[...] = jnp.zeros_like(tailv_ref)

    a = a_ref[...]

    def branch(wbf_ref, cw_ref, cb_ref, tail_ref):
        u = jnp.dot(a, wbf_ref[...], preferred_element_type=F32)
        tail = tail_ref[...]
        cw = cw_ref[...]
        kw = cw.shape[0]
        y = cw[kw - 1:kw, :] * u + cb_ref[...]
        for s in range(1, kw):
            y = y + cw[kw - 1 - s:kw - s, :] * _shift_rows_carry(u, s, tail)
        tail_ref[...] = u[bm - SUBLANES:bm, :]
        return y

    yg = branch(wgbf_ref, cwg_ref, cbg_ref, tailg_ref)
    yv = branch(wvbf_ref, cwv_ref, cbv_ref, tailv_ref)
    o_ref[...] = (yg * _sigmoid(yg) * yv).astype(o_ref.dtype)


def _ffn_up(hn, p, layer, seq):
    n, d = hn.shape
    d_ff = p["ffn_w_down"].shape[1]
    kw = p["ffn_conv_w"].shape[1]
    bn = _pick(d_ff, (256, 128))
    bm = _pick(seq, (1024, 512, 256))
    nj = d_ff // bn
    est = 2 * bm * d * 2 + 2 * 2 * d * bn * 4 + 2 * d * bn * 2 + 2 * bm * bn * 2 + 8 * bm * bn * 4
    kern = functools.partial(_ffn_up_kernel, tiles_per_seq=seq // bm, cast_chunk=_pick(d, (256, 128, 64, 8)))
    return pl.pallas_call(
        kern,
        grid=(nj, n // bm),
        in_specs=[pl.BlockSpec((bm, d), lambda j, i: (i, 0)),
                  pl.BlockSpec((None, d, bn), lambda j, i: (layer, 0, j)),
                  pl.BlockSpec((None, d, bn), lambda j, i: (layer, 0, j + nj)),
                  pl.BlockSpec((None, kw, bn), lambda j, i: (layer, 0, j)),
                  pl.BlockSpec((None, kw, bn), lambda j, i: (layer, 0, j + nj)),
                  pl.BlockSpec((None, 1, bn), lambda j, i: (layer, 0, j)),
                  pl.BlockSpec((None, 1, bn), lambda j, i: (layer, 0, j + nj))],
        out_specs=pl.BlockSpec((bm, bn), lambda j, i: (i, j)),
        out_shape=jax.ShapeDtypeStruct((n, d_ff), BF16),
        scratch_shapes=[pltpu.VMEM((d, bn), BF16), pltpu.VMEM((d, bn), BF16),
                        pltpu.VMEM((SUBLANES, bn), F32), pltpu.VMEM((SUBLANES, bn), F32)],
        compiler_params=pltpu.CompilerParams(dimension_semantics=("arbitrary", "arbitrary"),
                                             vmem_limit_bytes=min(est + VMEM_MARGIN_BYTES,
                                                                  VMEM_CAP_BYTES - VMEM_MARGIN_BYTES // 2)),
        name="ffn_up_conv_gate",
    )(hn, p["ffn_w_up"], p["ffn_w_up"], p["ffn_conv_w"], p["ffn_conv_w"], p["ffn_conv_b"], p["ffn_conv_b"])


def kernel(x, ln1_w, w_in, lru_conv_w, lru_conv_b, lru_gate_a_w, lru_gate_a_b, lru_gate_x_w, lru_gate_x_b,
           lru_lambda, hgrn_lb_logits, fox_f_bias, mix_norm_w, w_out, ln2_w, ffn_w_up, ffn_conv_w, ffn_conv_b,
           ffn_w_down, final_norm_w):
    batch, seq, d_model = x.shape
    depth = w_in.shape[0]
    n = batch * seq
    lru_w = lru_conv_w.shape[-1]
    hg_w = hgrn_lb_logits.shape[-1]
    fox_heads = fox_f_bias.shape[-1]
    fox_w = fox_heads * HEAD_DIM
    d_main = 2 * lru_w + 4 * hg_w + 3 * fox_w
    d_mix = lru_w + hg_w + fox_w
    d_ff = ffn_w_down.shape[1]
    assert w_in.shape[-1] == d_main + fox_heads and fox_heads <= LANES
    assert lru_w == hg_w, "column-block addressing of the projection assumes equal LRU / HGRN2 widths"

    row3 = lambda a: a.reshape(a.shape[0], 1, a.shape[-1])
    p = {
        "lru_conv_w": lru_conv_w, "lru_conv_b": row3(lru_conv_b),
        "lru_gate_a_w": lru_gate_a_w, "lru_gate_a_b": row3(lru_gate_a_b),
        "lru_gate_x_w": lru_gate_x_w, "lru_gate_x_b": row3(lru_gate_x_b),
        "lru_lambda": row3(lru_lambda), "hgrn_lb_logits": hgrn_lb_logits, "fox_f_bias": fox_f_bias,
        "mix_norm_w": row3(mix_norm_w), "ffn_w_up": ffn_w_up, "ffn_conv_w": ffn_conv_w,
        "ffn_conv_b": row3(ffn_conv_b), "ffn_w_down": ffn_w_down,
    }
    ln1 = row3(ln1_w)
    ln2 = row3(ln2_w)
    w_f = jnp.pad(w_in[:, :, d_main:], ((0, 0), (0, 0), (0, LANES - fox_heads)))
    f_bias = jnp.pad(fox_f_bias, ((0, 0), (0, LANES - fox_heads)))

    bm = _pick(n, (512, 256, 128))
    bn_in = _pick(d_main, (512, 256, 128))
    bn_out = _pick(d_model, (512, 256, 128))
    k_half = d_ff // 2
    assert k_half % LANES == 0

    xf = x.reshape(n, d_model)
    for l in range(depth):
        hn = _rmsnorm(xf, ln1, l, BF16)
        proj = _matmul([hn], w_in, w_lead=(l,), k_block=0, k_size=d_model, n_cols=d_main, bn=bn_in, bm=bm,
                       name="in_proj")
        flog = _matmul([hn], w_f, w_lead=(l,), k_block=0, k_size=d_model, n_cols=LANES, bn=LANES, bm=bm,
                       name="fox_forget_proj")
        cum, cumt = _fox_cum(flog, f_bias[l:l + 1], batch, seq)

        out_lru = _lru_group(proj, p, l, batch, seq)
        out_hg = _hgrn_group(proj, p, l, batch, seq, col0=2 * lru_w // hg_w)
        out_fox = _fox_group(proj, cum, cumt, p, l, batch, seq,
                             qcol0=(2 * lru_w + 4 * hg_w) // HEAD_DIM, gcol0=(lru_w + hg_w) // HEAD_DIM)

        xf = _matmul([out_lru, out_hg, out_fox], w_out, w_lead=(l,), k_block=0, k_size=d_mix, n_cols=d_model,
                     bn=bn_out, bm=bm, res=xf, name="out_proj")

        hn2 = _rmsnorm(xf, ln2, l, BF16)
        hmid = _ffn_up(hn2, p, l, seq)
        for kb in range(2):
            xf = _matmul([hmid], ffn_w_down, w_lead=(l,), k_block=kb, k_size=k_half, n_cols=d_model,
                         bn=bn_out, bm=bm, res=xf, name=f"ffn_down_{kb}")

    out = _rmsnorm(xf, final_norm_w.reshape(1, 1, d_model), 0, x.dtype)
    return out.reshape(batch, seq, d_model)
```

```python
import functools

import jax
import jax.numpy as jnp
from jax import lax
from jax.experimental import pallas as pl
from jax.experimental.pallas import tpu as pltpu

F32 = jnp.float32
BF16 = jnp.bfloat16

HEAD_DIM = 128
LRU_C = 8.0
EPS = 1e-6
NEG_BIG = -1e30
LOG2_E = 1.4426950408889634

SUBLANES = 8
LANES = 128
VMEM_CAP_BYTES = 64 * 1024 * 1024
VMEM_MARGIN_BYTES = 6 * 1024 * 1024

NT_DIMS = (((1,), (1,)), ((), ()))
FFN_ROW_CHUNK = 256


def _vmem_limit(estimate_bytes):
    return int(min(max(2 * estimate_bytes, 16 * 1024 * 1024), VMEM_CAP_BYTES - VMEM_MARGIN_BYTES))


def _params(semantics, vmem_estimate):
    return pltpu.CompilerParams(dimension_semantics=semantics, vmem_limit_bytes=_vmem_limit(vmem_estimate))


def _pick(n, candidates):
    for c in candidates:
        if n % c == 0:
            return c
    raise ValueError(f"no tile in {candidates} divides {n}")


def _row_iota(shape):
    return lax.broadcasted_iota(jnp.int32, shape, 0)


def _shift_rows(x, shift, fill):
    rolled = pltpu.roll(x, shift, 0)
    return jnp.where(_row_iota(x.shape) >= shift, rolled, fill)


def _shift_rows_carry(x, shift, tail):
    y = pltpu.roll(x, shift, 0)
    rows = _row_iota(x.shape)
    nt = tail.shape[0]
    for r in range(shift):
        y = jnp.where(rows == r, tail[nt - shift + r:nt - shift + r + 1, :], y)
    return y


def _sigmoid(x):
    return 1.0 / (1.0 + jnp.exp(-x))


def _log_sigmoid(x):
    return jnp.minimum(x, 0.0) - jnp.log1p(jnp.exp(-jnp.abs(x)))


def _softplus(x):
    return jnp.maximum(x, 0.0) + jnp.log1p(jnp.exp(-jnp.abs(x)))


def _head_rms(o, gain):
    var = jnp.mean(o * o, axis=-1, keepdims=True)
    return o * lax.rsqrt(var + EPS) * gain


def _rmsnorm_kernel(x_ref, g_ref, o_ref):
    x = x_ref[...]
    var = jnp.mean(x * x, axis=-1, keepdims=True)
    o_ref[...] = (x * lax.rsqrt(var + EPS) * g_ref[...]).astype(o_ref.dtype)


def _rmsnorm(x, gains, layer, out_dtype):
    n, d = x.shape
    rows = _pick(n, (256, 128, 64, 8))
    est = 2 * rows * d * (4 + jnp.dtype(out_dtype).itemsize)
    return pl.pallas_call(
        _rmsnorm_kernel,
        grid=(n // rows,),
        in_specs=[pl.BlockSpec((rows, d), lambda i: (i, 0)),
                  pl.BlockSpec((None, 1, d), lambda i: (layer, 0, 0))],
        out_specs=pl.BlockSpec((rows, d), lambda i: (i, 0)),
        out_shape=jax.ShapeDtypeStruct((n, d), out_dtype),
        compiler_params=_params(("arbitrary",), est),
        name="rmsnorm",
    )(x, gains)


def _cast_weight(w_ref, wbf_ref, chunk):
    def body(c, carry):
        r = pl.multiple_of(c * chunk, chunk)
        wbf_ref[pl.ds(r, chunk), :] = w_ref[pl.ds(r, chunk), :].astype(BF16)
        return carry
    lax.fori_loop(0, w_ref.shape[0] // chunk, body, 0)


def _cast_weight_transposed(wt_ref, wbf_ref, chunk):
    for c in range(wt_ref.shape[1] // chunk):
        wbf_ref[c * chunk:(c + 1) * chunk, :] = wt_ref[:, c * chunk:(c + 1) * chunk].T.astype(BF16)


def _mm_kernel(*refs, a_segments, has_res, cast_chunk, w_transposed):
    n_a = len(a_segments)
    a_refs = refs[:n_a]
    w_ref = refs[n_a]
    res_ref = refs[n_a + 1] if has_res else None
    o_ref = refs[n_a + 1 + has_res]
    wbf_ref = refs[n_a + 2 + has_res]

    @pl.when(pl.program_id(1) == 0)
    def _():
        if w_transposed:
            _cast_weight_transposed(w_ref, wbf_ref, cast_chunk)
        else:
            _cast_weight(w_ref, wbf_ref, cast_chunk)

    acc = None
    for a_ref, (off, width) in zip(a_refs, a_segments):
        part = jnp.dot(a_ref[...], wbf_ref[off:off + width, :], preferred_element_type=F32)
        acc = part if acc is None else acc + part
    if has_res:
        acc = acc + res_ref[...]
    o_ref[...] = acc.astype(o_ref.dtype)


def _matmul(a_list, w, *, w_lead, k_block, k_size, n_cols, col_block0=0, bn, bm, res=None,
            out_dtype=F32, w_transposed=False, name):
    n = a_list[0].shape[0]
    segs, off = [], 0
    for a in a_list:
        width = k_size if len(a_list) == 1 else a.shape[1]
        segs.append((off, width))
        off += width
    assert off == k_size
    lead = tuple(w_lead)
    if w_transposed:
        w_spec = pl.BlockSpec((None,) * len(lead) + (bn, k_size), lambda j, i: lead + (j + col_block0, k_block))
    else:
        w_spec = pl.BlockSpec((None,) * len(lead) + (k_size, bn), lambda j, i: lead + (k_block, j + col_block0))
    in_specs = [pl.BlockSpec((bm, width), lambda j, i, kb=(k_block if len(a_list) == 1 else 0): (i, kb))
                for (_, width) in segs]
    in_specs.append(w_spec)
    args = list(a_list) + [w]
    if res is not None:
        in_specs.append(pl.BlockSpec((bm, bn), lambda j, i: (i, j)))
        args.append(res)
    out_bytes = jnp.dtype(out_dtype).itemsize
    est = (2 * k_size * bn * 4 + k_size * bn * 2 + 2 * bm * k_size * 2 + 2 * bm * bn * out_bytes
           + (2 * bm * bn * 4 if res is not None else 0) + bm * bn * 4)
    kern = functools.partial(_mm_kernel, a_segments=tuple(segs), has_res=res is not None,
                             cast_chunk=_pick(k_size, (256, 128) if w_transposed else (256, 128, 64, 8)),
                             w_transposed=w_transposed)
    return pl.pallas_call(
        kern,
        grid=(n_cols // bn, n // bm),
        in_specs=in_specs,
        out_specs=pl.BlockSpec((bm, bn), lambda j, i: (i, j)),
        out_shape=jax.ShapeDtypeStruct((n, n_cols), out_dtype),
        scratch_shapes=[pltpu.VMEM((k_size, bn), BF16)],
        compiler_params=pltpu.CompilerParams(dimension_semantics=("arbitrary", "arbitrary"),
                                             vmem_limit_bytes=min(est + VMEM_MARGIN_BYTES,
                                                                  VMEM_CAP_BYTES - VMEM_MARGIN_BYTES // 2)),
        name=name,
    )(*args)


def _lru_kernel(x_ref, y_ref, cw_ref, cb_ref, wa_ref, ba_ref, wx_ref, bx_ref, lam_ref, gain_ref,
                o_ref, xtail_ref, hcar_ref):
    t, width = x_ref.shape
    nblk = width // HEAD_DIM

    @pl.when(pl.program_id(1) == 0)
    def _():
        xtail_ref[...] = jnp.zeros_like(xtail_ref)
        hcar_ref[...] = jnp.zeros_like(hcar_ref)

    x = x_ref[...]
    tail = xtail_ref[...]
    cw = cw_ref[...]
    kw = cw.shape[0]
    xc = cw[kw - 1:kw, :] * x + cb_ref[...]
    for s in range(1, kw):
        xc = xc + cw[kw - 1 - s:kw - s, :] * _shift_rows_carry(x, s, tail)
    xtail_ref[...] = x[t - SUBLANES:t, :]

    xcb = xc.astype(BF16)
    ra, rx = [], []
    for n in range(nblk):
        xs = xcb[:, n * HEAD_DIM:(n + 1) * HEAD_DIM]
        ra.append(jnp.dot(xs, wa_ref[n].astype(BF16), preferred_element_type=F32))
        rx.append(jnp.dot(xs, wx_ref[n].astype(BF16), preferred_element_type=F32))
    r = _sigmoid(jnp.concatenate(ra, axis=1) + ba_ref[...])
    gi = _sigmoid(jnp.concatenate(rx, axis=1) + bx_ref[...])
    log_a = (-LRU_C) * r * _softplus(-lam_ref[...])
    a = jnp.exp(log_a)
    u = jnp.sqrt(-jnp.tanh(log_a) * (a * a + 1.0)) * (gi * xc)

    sh = 1
    while sh < t:
        u = u + a * _shift_rows(u, sh, 0.0)
        a = a * _shift_rows(a, sh, 1.0)
        sh *= 2
    h = u + a * hcar_ref[0:1, :]
    hcar_ref[...] = jnp.broadcast_to(h[t - 1:t, :], hcar_ref.shape)

    gate = jax.nn.gelu(y_ref[...])
    gain = gain_ref[...]
    for n in range(nblk):
        sl = slice(n * HEAD_DIM, (n + 1) * HEAD_DIM)
        o_ref[:, sl] = (_head_rms(h[:, sl], gain[:, sl]) * gate[:, sl]).astype(o_ref.dtype)


def _lru_group(proj, p, layer, batch, seq):
    width = p["lru_conv_w"].shape[-1]
    nblk = width // HEAD_DIM
    t = _pick(seq, (256, 128, 64, 8))
    nc = seq // t
    kw = p["lru_conv_w"].shape[1]
    row = lambda b, c: (b * nc + c, 0)
    vec = lambda b, c: (layer, 0, 0)
    est = 2 * 2 * t * width * 4 + 2 * t * width * 2 + 16 * t * width * 4
    return pl.pallas_call(
        _lru_kernel,
        grid=(batch, nc),
        in_specs=[pl.BlockSpec((t, width), row),
                  pl.BlockSpec((t, width), lambda b, c: (b * nc + c, 1)),
                  pl.BlockSpec((None, kw, width), vec),
                  pl.BlockSpec((None, 1, width), vec),
                  pl.BlockSpec((None, nblk, HEAD_DIM, HEAD_DIM), lambda b, c: (layer, 0, 0, 0)),
                  pl.BlockSpec((None, 1, width), vec),
                  pl.BlockSpec((None, nblk, HEAD_DIM, HEAD_DIM), lambda b, c: (layer, 0, 0, 0)),
                  pl.BlockSpec((None, 1, width), vec),
                  pl.BlockSpec((None, 1, width), vec),
                  pl.BlockSpec((None, 1, width), vec)],
        out_specs=pl.BlockSpec((t, width), row),
        out_shape=jax.ShapeDtypeStruct((batch * seq, width), BF16),
        scratch_shapes=[pltpu.VMEM((SUBLANES, width), F32), pltpu.VMEM((SUBLANES, width), F32)],
        compiler_params=_params(("arbitrary", "arbitrary"), est),
        name="rg_lru",
    )(proj, proj, p["lru_conv_w"], p["lru_conv_b"], p["lru_gate_a_w"], p["lru_gate_a_b"],
      p["lru_gate_x_w"], p["lru_gate_x_b"], p["lru_lambda"], p["mix_norm_w"])


def _hgrn_kernel(q_ref, f_ref, i_ref, g_ref, lbl_ref, gain_ref, o_ref, state_ref, *, layer):
    t, width = q_ref.shape
    nheads = width // HEAD_DIM

    @pl.when(pl.program_id(1) == 0)
    def _():
        state_ref[...] = jnp.zeros_like(state_ref)

    lbl = lbl_ref[...]
    pe = jnp.exp(lbl - jnp.max(lbl, axis=0, keepdims=True))
    psm = pe / jnp.sum(pe, axis=0, keepdims=True)
    lb = jnp.zeros((1, width), F32)
    for d in range(1, layer + 1):
        lb = lb + psm[d:d + 1, :]

    z = f_ref[...]
    la = jnp.log(lb)
    lbz = jnp.log1p(-lb) + _log_sigmoid(z)
    mx = jnp.maximum(la, lbz)
    log_f = mx + jnp.log(jnp.exp(la - mx) + jnp.exp(lbz - mx))
    k_all = (1.0 - lb) * _sigmoid(-z)
    qz = q_ref[...]
    q_all = qz * _sigmoid(qz)
    v_all = i_ref[...]
    gate = _sigmoid(g_ref[...])
    gain = gain_ref[...]

    rows = _row_iota((t, 1))
    rr = _row_iota((t, t))
    cc = lax.broadcasted_iota(jnp.int32, (t, t), 1)

    for h in range(nheads):
        sl = slice(h * HEAD_DIM, (h + 1) * HEAD_DIM)
        q, k, v = q_all[:, sl], k_all[:, sl], v_all[:, sl]
        cum = log_f[:, sl]
        sh = 1
        while sh < t:
            cum = cum + _shift_rows(cum, sh, 0.0)
            sh *= 2
        last = cum[t - 1:t, :]
        st = state_ref[h]
        o = lax.dot_general((q * jnp.exp(cum)).astype(BF16), st.astype(BF16), NT_DIMS,
                            preferred_element_type=F32)

        s_mat = jnp.zeros((t, t), F32)
        m = SUBLANES
        while 2 * m <= t:
            cref = jnp.concatenate(
                [jnp.broadcast_to(cum[b * 2 * m + m - 1:b * 2 * m + m, :], (2 * m, HEAD_DIM))
                 for b in range(t // (2 * m))], axis=0)
            e = jnp.exp(-jnp.abs(cum - cref))
            qrole = (rows % (2 * m)) >= m
            qp = jnp.where(qrole, q * e, 0.0).astype(BF16)
            kp = jnp.where(qrole, 0.0, k * e).astype(BF16)
            s_l = lax.dot_general(qp, kp, NT_DIMS, preferred_element_type=F32)
            s_mat = s_mat + jnp.where((rr // (2 * m)) == (cc // (2 * m)), s_l, 0.0)
            m *= 2
        for d in range(SUBLANES):
            if d == 0:
                prod = q * k
            else:
                dec = jnp.exp(jnp.minimum(cum - pltpu.roll(cum, d, 0), 0.0))
                prod = q * pltpu.roll(k, d, 0) * dec
            c = jnp.sum(prod, axis=-1, keepdims=True)
            s_mat = s_mat + jnp.where((cc == rr - d) & ((rr % SUBLANES) >= d), c, 0.0)
        o = o + jnp.dot(s_mat.astype(BF16), v.astype(BF16), preferred_element_type=F32)

        kd = (k * jnp.exp(last - cum)).astype(BF16)
        state_ref[h] = st * jnp.exp(last) + jnp.dot(v.T.astype(BF16), kd, preferred_element_type=F32)

        o_ref[:, sl] = (_head_rms(o, gain[:, sl]) * gate[:, sl]).astype(o_ref.dtype)


def _hgrn_group(proj, p, layer, batch, seq, col0):
    depth, width = p["hgrn_lb_logits"].shape
    nheads = width // HEAD_DIM
    t = _pick(seq, (64, 32, 16))
    nc = seq // t
    blk = lambda k: pl.BlockSpec((t, width), lambda b, c, k=k: (b * nc + c, col0 + k))
    gcol = p["lru_conv_w"].shape[-1] // width
    est = 2 * 4 * t * width * 4 + 2 * t * width * 2 + nheads * HEAD_DIM * HEAD_DIM * 4 + 24 * t * width * 4
    return pl.pallas_call(
        functools.partial(_hgrn_kernel, layer=layer),
        grid=(batch, nc),
        in_specs=[blk(0), blk(1), blk(2), blk(3),
                  pl.BlockSpec((depth, width), lambda b, c: (0, 0)),
                  pl.BlockSpec((None, 1, width), lambda b, c: (layer, 0, gcol))],
        out_specs=pl.BlockSpec((t, width), lambda b, c: (b * nc + c, 0)),
        out_shape=jax.ShapeDtypeStruct((batch * seq, width), BF16),
        scratch_shapes=[pltpu.VMEM((nheads, HEAD_DIM, HEAD_DIM), F32)],
        compiler_params=_params(("arbitrary", "arbitrary"), est),
        name="hgrn2",
    )(proj, proj, proj, proj, p["hgrn_lb_logits"], p["mix_norm_w"])


def _fox_cum_kernel(f_ref, b_ref, cum_ref, cumt_ref):
    s = f_ref.shape[0]
    cum = _log_sigmoid(f_ref[...] + b_ref[...])
    sh = 1
    while sh < s:
        cum = cum + _shift_rows(cum, sh, 0.0)
        sh *= 2
    cum = cum * LOG2_E
    cum_ref[...] = cum
    cumt_ref[...] = cum.T


def _fox_cum(flog, bias_row, batch, seq):
    est = 2 * 3 * seq * LANES * 4 + 8 * seq * LANES * 4
    return pl.pallas_call(
        _fox_cum_kernel,
        grid=(batch,),
        in_specs=[pl.BlockSpec((seq, LANES), lambda b: (b, 0)),
                  pl.BlockSpec((1, LANES), lambda b: (0, 0))],
        out_specs=[pl.BlockSpec((seq, LANES), lambda b: (b, 0)),
                   pl.BlockSpec((None, LANES, seq), lambda b: (b, 0, 0))],
        out_shape=[jax.ShapeDtypeStruct((batch * seq, LANES), F32),
                   jax.ShapeDtypeStruct((batch, LANES, seq), F32)],
        compiler_params=_params(("arbitrary",), est),
        name="fox_cum",
    )(flog, bias_row)


def _fox_kernel(q_ref, k_ref, v_ref, cq_ref, ck_ref, gain_ref, o_ref, kbf_ref, vbf_ref, s_ref, m_ref, acc_ref):
    tq = q_ref.shape[0]
    head = pl.program_id(1)
    qi = pl.program_id(2)

    @pl.when(qi == 0)
    def _():
        _cast_weight(k_ref, kbf_ref, tq)

        def body(c, carry):
            r = pl.multiple_of(c * tq, tq)
            vbf_ref[pl.ds(r, tq), 0:HEAD_DIM] = v_ref[pl.ds(r, tq), :].astype(BF16)
            vbf_ref[pl.ds(r, tq), HEAD_DIM:2 * HEAD_DIM] = jnp.ones((tq, HEAD_DIM), BF16)
            return carry
        lax.fori_loop(0, v_ref.shape[0] // tq, body, 0)

    q = (q_ref[...] * (HEAD_DIM ** -0.5 * LOG2_E)).astype(BF16)
    lane = lax.broadcasted_iota(jnp.int32, (1, LANES), 1)
    cq = jnp.sum(jnp.where(lane == head, cq_ref[...], 0.0), axis=-1, keepdims=True)

    m_ref[...] = jnp.full_like(m_ref, NEG_BIG)

    def logits(ki, masked):
        r0 = pl.multiple_of(ki * tq, tq)
        s = lax.dot_general(q, kbf_ref[pl.ds(r0, tq), :], NT_DIMS, preferred_element_type=F32)
        s = (s + cq) - ck_ref[pl.ds(ki, 1), :]
        if masked:
            s = jnp.where(_row_iota((tq, tq)) >= lax.broadcasted_iota(jnp.int32, (tq, tq), 1), s, NEG_BIG)
        s_ref[ki] = s
        part = s[:, 0:LANES]
        for c in range(1, tq // LANES):
            part = jnp.maximum(part, s[:, c * LANES:(c + 1) * LANES])
        m_ref[...] = jnp.maximum(m_ref[...], part)

    def pass1(ki, carry):
        logits(ki, False)
        return carry

    lax.fori_loop(0, qi, pass1, 0)
    logits(qi, True)

    m = jnp.max(m_ref[...], axis=-1, keepdims=True)
    acc_ref[...] = jnp.zeros_like(acc_ref)

    def pass2(ki, carry):
        r0 = pl.multiple_of(ki * tq, tq)
        pr = jnp.exp2(s_ref[ki] - m).astype(BF16)
        acc_ref[...] += jnp.dot(pr, vbf_ref[pl.ds(r0, tq), :], preferred_element_type=F32)
        return carry

    lax.fori_loop(0, qi + 1, pass2, 0)

    acc = acc_ref[...]
    o = acc[:, 0:HEAD_DIM] / acc[:, HEAD_DIM:HEAD_DIM + 1]
    o_ref[...] = _head_rms(o, gain_ref[...]).astype(o_ref.dtype)


def _fox_group(proj, cum, cumt, p, layer, batch, seq, qcol0, gcol0):
    nheads = p["fox_f_bias"].shape[-1]
    tq = _pick(seq, (1024, 512, 256, 128))
    nq = seq // tq
    cumt4 = cumt.reshape(batch, LANES, nq, tq)
    est = (2 * tq * HEAD_DIM * 4 + 2 * 2 * seq * HEAD_DIM * 4 + 3 * seq * HEAD_DIM * 2 + 2 * tq * LANES * 4
           + 2 * nq * tq * 4 + 2 * tq * HEAD_DIM * 2 + 3 * tq * LANES * 4 + (nq + 4) * tq * tq * 4)
    return pl.pallas_call(
        _fox_kernel,
        grid=(batch, nheads, nq),
        in_specs=[pl.BlockSpec((tq, HEAD_DIM), lambda b, h, i: (b * nq + i, qcol0 + h)),
                  pl.BlockSpec((seq, HEAD_DIM), lambda b, h, i: (b, qcol0 + nheads + h)),
                  pl.BlockSpec((seq, HEAD_DIM), lambda b, h, i: (b, qcol0 + 2 * nheads + h)),
                  pl.BlockSpec((tq, LANES), lambda b, h, i: (b * nq + i, 0)),
                  pl.BlockSpec((None, None, nq, tq), lambda b, h, i: (b, h, 0, 0)),
                  pl.BlockSpec((None, 1, HEAD_DIM), lambda b, h, i: (layer, 0, gcol0 + h))],
        out_specs=pl.BlockSpec((tq, HEAD_DIM), lambda b, h, i: (b * nq + i, h)),
        out_shape=jax.ShapeDtypeStruct((batch * seq, nheads * HEAD_DIM), BF16),
        scratch_shapes=[pltpu.VMEM((seq, HEAD_DIM), BF16), pltpu.VMEM((seq, 2 * HEAD_DIM), BF16),
                        pltpu.VMEM((nq, tq, tq), F32), pltpu.VMEM((tq, LANES), F32),
                        pltpu.VMEM((tq, 2 * HEAD_DIM), F32)],
        compiler_params=_params(("arbitrary", "arbitrary", "arbitrary"), est),
        name="fox_attention",
    )(proj, proj, proj, cum, cumt4, p["mix_norm_w"])


def _ffn_up_kernel(a_ref, wg_ref, wv_ref, cwg_ref, cwv_ref, cbg_ref, cbv_ref, o_ref,
                   wgbf_ref, wvbf_ref, tailg_ref, tailv_ref, *, tiles_per_seq, cast_chunk, row_chunk):
    i = pl.program_id(1)
    bm = a_ref.shape[0]

    @pl.when(i == 0)
    def _():
        _cast_weight(wg_ref, wgbf_ref, cast_chunk)
        _cast_weight(wv_ref, wvbf_ref, cast_chunk)

    @pl.when(i % tiles_per_seq == 0)
    def _():
        tailg_ref[...] = jnp.zeros_like(tailg_ref)
        tailv_ref[...] = jnp.zeros_like(tailv_ref)

    def conv(u, tail, cw, cb):
        kw = cw.shape[0]
        ext = jnp.concatenate([tail, u[0:SUBLANES, :]], axis=0)
        y = cw[kw - 1:kw, :] * u + cb
        ytop = cw[kw - 1:kw, :] * u[0:SUBLANES, :] + cb
        for s in range(1, kw):
            tap = cw[kw - 1 - s:kw - s, :]
            y = y + tap * pltpu.roll(u, s, 0)
            ytop = ytop + tap * pltpu.roll(ext, s, 0)[SUBLANES:2 * SUBLANES, :]
        return jnp.concatenate([ytop, y[SUBLANES:, :]], axis=0)

    cwg, cwv, cbg, cbv = cwg_ref[...], cwv_ref[...], cbg_ref[...], cbv_ref[...]
    tg, tv = tailg_ref[...], tailv_ref[...]
    for c in range(bm // row_chunk):
        rows = slice(c * row_chunk, (c + 1) * row_chunk)
        a = a_ref[rows, :]
        ug = jnp.dot(a, wgbf_ref[...], preferred_element_type=F32)
        uv = jnp.dot(a, wvbf_ref[...], preferred_element_type=F32)
        yg = conv(ug, tg, cwg, cbg)
        yv = conv(uv, tv, cwv, cbv)
        o_ref[rows, :] = (yg * _sigmoid(yg) * yv).astype(o_ref.dtype)
        tg, tv = ug[row_chunk - SUBLANES:, :], uv[row_chunk - SUBLANES:, :]
    tailg_ref[...] = tg
    tailv_ref[...] = tv


def _ffn_up(hn, p, layer, seq):
    n, d = hn.shape
    d_ff = p["ffn_w_down"].shape[1]
    kw = p["ffn_conv_w"].shape[1]
    bn = _pick(d_ff, (256, 128))
    bm = _pick(seq, (1024, 512, 256))
    nj = d_ff // bn
    est = 2 * bm * d * 2 + 2 * 2 * d * bn * 4 + 2 * d * bn * 2 + 2 * bm * bn * 2 + 8 * bm * bn * 4
    kern = functools.partial(_ffn_up_kernel, tiles_per_seq=seq // bm, cast_chunk=_pick(d, (256, 128, 64, 8)),
                             row_chunk=min(bm, FFN_ROW_CHUNK))
    return pl.pallas_call(
        kern,
        grid=(nj, n // bm),
        in_specs=[pl.BlockSpec((bm, d), lambda j, i: (i, 0)),
                  pl.BlockSpec((None, d, bn), lambda j, i: (layer, 0, j)),
                  pl.BlockSpec((None, d, bn), lambda j, i: (layer, 0, j + nj)),
                  pl.BlockSpec((None, kw, bn), lambda j, i: (layer, 0, j)),
                  pl.BlockSpec((None, kw, bn), lambda j, i: (layer, 0, j + nj)),
                  pl.BlockSpec((None, 1, bn), lambda j, i: (layer, 0, j)),
                  pl.BlockSpec((None, 1, bn), lambda j, i: (layer, 0, j + nj))],
        out_specs=pl.BlockSpec((bm, bn), lambda j, i: (i, j)),
        out_shape=jax.ShapeDtypeStruct((n, d_ff), BF16),
        scratch_shapes=[pltpu.VMEM((d, bn), BF16), pltpu.VMEM((d, bn), BF16),
                        pltpu.VMEM((SUBLANES, bn), F32), pltpu.VMEM((SUBLANES, bn), F32)],
        compiler_params=pltpu.CompilerParams(dimension_semantics=("arbitrary", "arbitrary"),
                                             vmem_limit_bytes=min(est + VMEM_MARGIN_BYTES,
                                                                  VMEM_CAP_BYTES - VMEM_MARGIN_BYTES // 2)),
        name="ffn_up_conv_gate",
    )(hn, p["ffn_w_up"], p["ffn_w_up"], p["ffn_conv_w"], p["ffn_conv_w"], p["ffn_conv_b"], p["ffn_conv_b"])


def kernel(x, ln1_w, w_in, lru_conv_w, lru_conv_b, lru_gate_a_w, lru_gate_a_b, lru_gate_x_w, lru_gate_x_b,
           lru_lambda, hgrn_lb_logits, fox_f_bias, mix_norm_w, w_out, ln2_w, ffn_w_up, ffn_conv_w, ffn_conv_b,
           ffn_w_down, final_norm_w):
    batch, seq, d_model = x.shape
    depth = w_in.shape[0]
    n = batch * seq
    lru_w = lru_conv_w.shape[-1]
    hg_w = hgrn_lb_logits.shape[-1]
    fox_heads = fox_f_bias.shape[-1]
    fox_w = fox_heads * HEAD_DIM
    d_main = 2 * lru_w + 4 * hg_w + 3 * fox_w
    d_mix = lru_w + hg_w + fox_w
    d_ff = ffn_w_down.shape[1]
    assert w_in.shape[-1] == d_main + fox_heads and fox_heads <= LANES
    assert lru_w == hg_w, "column-block addressing of the projection assumes equal LRU / HGRN2 widths"

    row3 = lambda a: a.reshape(a.shape[0], 1, a.shape[-1])
    p = {
        "lru_conv_w": lru_conv_w, "lru_conv_b": row3(lru_conv_b),
        "lru_gate_a_w": lru_gate_a_w, "lru_gate_a_b": row3(lru_gate_a_b),
        "lru_gate_x_w": lru_gate_x_w, "lru_gate_x_b": row3(lru_gate_x_b),
        "lru_lambda": row3(lru_lambda), "hgrn_lb_logits": hgrn_lb_logits, "fox_f_bias": fox_f_bias,
        "mix_norm_w": row3(mix_norm_w), "ffn_w_up": ffn_w_up, "ffn_conv_w": ffn_conv_w,
        "ffn_conv_b": row3(ffn_conv_b), "ffn_w_down": ffn_w_down,
    }
    ln1 = row3(ln1_w)
    ln2 = row3(ln2_w)
    w_in_t = jnp.swapaxes(w_in, 1, 2)
    w_f_t = jnp.pad(w_in_t[:, d_main:, :], ((0, 0), (0, LANES - fox_heads), (0, 0)))
    f_bias = jnp.pad(fox_f_bias, ((0, 0), (0, LANES - fox_heads)))

    bm = _pick(n, (512, 256, 128))
    bm_big = _pick(n, (1024, 512, 256, 128))
    bn_in = _pick(d_main, (512, 256, 128))
    bn_out = _pick(d_model, (512, 256, 128))
    k_half = d_ff // 2
    assert k_half % LANES == 0

    xf = x.reshape(n, d_model)
    for l in range(depth):
        hn = _rmsnorm(xf, ln1, l, BF16)
        proj = _matmul([hn], w_in_t, w_lead=(l,), k_block=0, k_size=d_model, n_cols=d_main, bn=bn_in, bm=bm_big,
                       w_transposed=True, name="in_proj")
        flog = _matmul([hn], w_f_t, w_lead=(l,), k_block=0, k_size=d_model, n_cols=LANES, bn=LANES, bm=bm_big,
                       w_transposed=True, name="fox_forget_proj")
        cum, cumt = _fox_cum(flog, f_bias[l:l + 1], batch, seq)

        out_lru = _lru_group(proj, p, l, batch, seq)
        out_hg = _hgrn_group(proj, p, l, batch, seq, col0=2 * lru_w // hg_w)
        out_fox = _fox_group(proj, cum, cumt, p, l, batch, seq,
                             qcol0=(2 * lru_w + 4 * hg_w) // HEAD_DIM, gcol0=(lru_w + hg_w) // HEAD_DIM)

        xf = _matmul([out_lru, out_hg, out_fox], w_out, w_lead=(l,), k_block=0, k_size=d_mix, n_cols=d_model,
                     bn=bn_out, bm=bm_big, res=xf, name="out_proj")

        hn2 = _rmsnorm(xf, ln2, l, BF16)
        hmid = _ffn_up(hn2, p, l, seq)
        for kb in range(2):
            xf = _matmul([hmid], ffn_w_down, w_lead=(l,), k_block=kb, k_size=k_half, n_cols=d_model,
                         bn=bn_out, bm=bm, res=xf, name=f"ffn_down_{kb}")

    out = _rmsnorm(xf, final_norm_w.reshape(1, 1, d_model), 0, x.dtype)
    return out.reshape(batch, seq, d_model)
```

```python
import functools

import jax
import jax.numpy as jnp
from jax import lax
from jax.experimental import pallas as pl
from jax.experimental.pallas import tpu as pltpu

F32 = jnp.float32
BF16 = jnp.bfloat16

HEAD_DIM = 128
LRU_C = 8.0
EPS = 1e-6
NEG_BIG = -1e30
LOG2_E = 1.4426950408889634

SUBLANES = 8
LANES = 128
VMEM_CAP_BYTES = 64 * 1024 * 1024
VMEM_MARGIN_BYTES = 6 * 1024 * 1024

NT_DIMS = (((1,), (1,)), ((), ()))
FFN_ROW_CHUNK = 256


def _vmem_limit(estimate_bytes):
    return int(min(max(2 * estimate_bytes, 16 * 1024 * 1024), VMEM_CAP_BYTES - VMEM_MARGIN_BYTES))


def _params(semantics, vmem_estimate):
    return pltpu.CompilerParams(dimension_semantics=semantics, vmem_limit_bytes=_vmem_limit(vmem_estimate))


def _pick(n, candidates):
    for c in candidates:
        if n % c == 0:
            return c
    raise ValueError(f"no tile in {candidates} divides {n}")


def _row_iota(shape):
    return lax.broadcasted_iota(jnp.int32, shape, 0)


def _shift_rows(x, shift, fill):
    rolled = pltpu.roll(x, shift, 0)
    return jnp.where(_row_iota(x.shape) >= shift, rolled, fill)


def _cumsum_rows(x):
    t, w = x.shape
    ngroups = t // SUBLANES
    x3 = x.reshape(ngroups, SUBLANES, w)
    pos = lax.broadcasted_iota(jnp.int32, (1, SUBLANES, 1), 1)
    sh = 1
    while sh < SUBLANES:
        x3 = x3 + jnp.where(pos >= sh, pltpu.roll(x3, sh, 1), 0.0)
        sh *= 2
    groups = [x3[0]]
    for g in range(1, ngroups):
        groups.append(x3[g] + groups[-1][SUBLANES - 1:SUBLANES, :])
    return jnp.concatenate(groups, axis=0)


def _sigmoid(x):
    return 0.5 * jnp.tanh(0.5 * x) + 0.5


def _log_sigmoid(x):
    return jnp.minimum(x, 0.0) - jnp.log1p(jnp.exp(-jnp.abs(x)))


def _softplus(x):
    return jnp.maximum(x, 0.0) + jnp.log1p(jnp.exp(-jnp.abs(x)))


def _head_rms(o, gain):
    var = jnp.mean(o * o, axis=-1, keepdims=True)
    return o * lax.rsqrt(var + EPS) * gain


def _rmsnorm_kernel(x_ref, g_ref, o_ref):
    x = x_ref[...]
    var = jnp.mean(x * x, axis=-1, keepdims=True)
    o_ref[...] = (x * lax.rsqrt(var + EPS) * g_ref[...]).astype(o_ref.dtype)


def _rmsnorm(x, gains, layer, out_dtype):
    n, d = x.shape
    rows = _pick(n, (512, 256, 128, 64, 8))
    est = 2 * rows * d * (4 + jnp.dtype(out_dtype).itemsize) + 2 * rows * d * 4
    return pl.pallas_call(
        _rmsnorm_kernel,
        grid=(n // rows,),
        in_specs=[pl.BlockSpec((rows, d), lambda i: (i, 0)),
                  pl.BlockSpec((None, 1, d), lambda i: (layer, 0, 0))],
        out_specs=pl.BlockSpec((rows, d), lambda i: (i, 0)),
        out_shape=jax.ShapeDtypeStruct((n, d), out_dtype),
        compiler_params=_params(("arbitrary",), est),
        name="rmsnorm",
    )(x, gains)


def _cast_weight(w_ref, wbf_ref, chunk, col0=0):
    cols = w_ref.shape[1]

    def body(c, carry):
        r = pl.multiple_of(c * chunk, chunk)
        wbf_ref[pl.ds(r, chunk), col0:col0 + cols] = w_ref[pl.ds(r, chunk), :].astype(BF16)
        return carry
    lax.fori_loop(0, w_ref.shape[0] // chunk, body, 0)


def _cast_weight_transposed(wt_ref, wbf_ref, chunk):
    for c in range(wt_ref.shape[1] // chunk):
        wbf_ref[c * chunk:(c + 1) * chunk, :] = wt_ref[:, c * chunk:(c + 1) * chunk].T.astype(BF16)


def _mm_kernel(*refs, a_segments, has_res, cast_chunk, w_transposed):
    n_a = len(a_segments)
    a_refs = refs[:n_a]
    w_ref = refs[n_a]
    res_ref = refs[n_a + 1] if has_res else None
    o_ref = refs[n_a + 1 + has_res]
    wbf_ref = refs[n_a + 2 + has_res]

    @pl.when(pl.program_id(1) == 0)
    def _():
        if w_transposed:
            _cast_weight_transposed(w_ref, wbf_ref, cast_chunk)
        else:
            _cast_weight(w_ref, wbf_ref, cast_chunk)

    acc = None
    for a_ref, (off, width) in zip(a_refs, a_segments):
        part = jnp.dot(a_ref[...], wbf_ref[off:off + width, :], preferred_element_type=F32)
        acc = part if acc is None else acc + part
    if has_res:
        acc = acc + res_ref[...]
    o_ref[...] = acc.astype(o_ref.dtype)


def _matmul(a_list, w, *, w_lead, k_block, k_size, n_cols, col_block0=0, bn, bm, res=None,
            out_dtype=F32, w_transposed=False, name):
    n = a_list[0].shape[0]
    segs, off = [], 0
    for a in a_list:
        width = k_size if len(a_list) == 1 else a.shape[1]
        segs.append((off, width))
        off += width
    assert off == k_size
    lead = tuple(w_lead)
    if w_transposed:
        w_spec = pl.BlockSpec((None,) * len(lead) + (bn, k_size), lambda j, i: lead + (j + col_block0, k_block))
    else:
        w_spec = pl.BlockSpec((None,) * len(lead) + (k_size, bn), lambda j, i: lead + (k_block, j + col_block0))
    in_specs = [pl.BlockSpec((bm, width), lambda j, i, kb=(k_block if len(a_list) == 1 else 0): (i, kb))
                for (_, width) in segs]
    in_specs.append(w_spec)
    args = list(a_list) + [w]
    if res is not None:
        in_specs.append(pl.BlockSpec((bm, bn), lambda j, i: (i, j)))
        args.append(res)
    out_bytes = jnp.dtype(out_dtype).itemsize
    est = (2 * k_size * bn * 4 + k_size * bn * 2 + 2 * bm * k_size * 2 + 2 * bm * bn * out_bytes
           + (2 * bm * bn * 4 if res is not None else 0) + bm * bn * 4)
    kern = functools.partial(_mm_kernel, a_segments=tuple(segs), has_res=res is not None,
                             cast_chunk=_pick(k_size, (256, 128) if w_transposed else (256, 128, 64, 8)),
                             w_transposed=w_transposed)
    return pl.pallas_call(
        kern,
        grid=(n_cols // bn, n // bm),
        in_specs=in_specs,
        out_specs=pl.BlockSpec((bm, bn), lambda j, i: (i, j)),
        out_shape=jax.ShapeDtypeStruct((n, n_cols), out_dtype),
        scratch_shapes=[pltpu.VMEM((k_size, bn), BF16)],
        compiler_params=pltpu.CompilerParams(dimension_semantics=("arbitrary", "arbitrary"),
                                             vmem_limit_bytes=min(est + VMEM_MARGIN_BYTES,
                                                                  VMEM_CAP_BYTES - VMEM_MARGIN_BYTES // 2)),
        name=name,
    )(*args)


def _mm_stream_kernel(*refs, a_segments, has_res, w_transposed):
    n_a = len(a_segments)
    a_refs = refs[:n_a]
    w_ref = refs[n_a]
    res_ref = refs[n_a + 1] if has_res else None
    o_ref = refs[n_a + 1 + has_res]
    wbf_ref = refs[n_a + 2 + has_res]
    jj = pl.program_id(0)
    slot = jj % 2
    ck = w_ref.shape[1] if w_transposed else w_ref.shape[0]
    r = pl.multiple_of(pl.program_id(1) * ck, ck)
    chunk = w_ref[...].T if w_transposed else w_ref[...]
    wbf_ref[slot, pl.ds(r, ck), :] = chunk.astype(BF16)

    @pl.when(jj > 0)
    def _():
        acc = None
        for a_ref, (off, width) in zip(a_refs, a_segments):
            part = jnp.dot(a_ref[...], wbf_ref[1 - slot, off:off + width, :], preferred_element_type=F32)
            acc = part if acc is None else acc + part
        if has_res:
            acc = acc + res_ref[...]
        o_ref[...] = acc.astype(o_ref.dtype)


def _matmul_stream(a_list, w, *, w_lead, k_block, k_size, n_cols, col_block0=0, bn, bm, res=None,
                   out_dtype=F32, w_transposed=False, name):
    n = a_list[0].shape[0]
    segs, off = [], 0
    for a in a_list:
        width = k_size if len(a_list) == 1 else a.shape[1]
        segs.append((off, width))
        off += width
    assert off == k_size
    ni, nj = n // bm, n_cols // bn
    assert k_size % ni == 0
    ck = k_size // ni
    assert ck % (LANES if w_transposed else 2 * SUBLANES) == 0
    lead = tuple(w_lead)
    row = lambda jj, i: jnp.where(jj == 0, 0, i)
    col = lambda jj: jnp.maximum(jj - 1, 0)
    nxt = lambda jj: jnp.minimum(jj, nj - 1) + col_block0
    if w_transposed:
        w_spec = pl.BlockSpec((None,) * len(lead) + (bn, ck), lambda jj, i: lead + (nxt(jj), k_block * ni + i))
    else:
        w_spec = pl.BlockSpec((None,) * len(lead) + (ck, bn), lambda jj, i: lead + (k_block * ni + i, nxt(jj)))
    in_specs = [pl.BlockSpec((bm, width), lambda jj, i, kb=(k_block if len(a_list) == 1 else 0): (row(jj, i), kb))
                for (_, width) in segs]
    in_specs.append(w_spec)
    args = list(a_list) + [w]
    if res is not None:
        in_specs.append(pl.BlockSpec((bm, bn), lambda jj, i: (row(jj, i), col(jj))))
        args.append(res)
    out_bytes = jnp.dtype(out_dtype).itemsize
    est = (2 * k_size * bn * 2 + 3 * ck * bn * 4 + 2 * bm * k_size * 2 + 2 * bm * bn * out_bytes
           + (2 * bm * bn * 4 if res is not None else 0) + 2 * bm * bn * 4)
    kern = functools.partial(_mm_stream_kernel, a_segments=tuple(segs), has_res=res is not None,
                             w_transposed=w_transposed)
    return pl.pallas_call(
        kern,
        grid=(nj + 1, ni),
        in_specs=in_specs,
        out_specs=pl.BlockSpec((bm, bn), lambda jj, i: (row(jj, i), col(jj))),
        out_shape=jax.ShapeDtypeStruct((n, n_cols), out_dtype),
        scratch_shapes=[pltpu.VMEM((2, k_size, bn), BF16)],
        compiler_params=pltpu.CompilerParams(dimension_semantics=("arbitrary", "arbitrary"),
                                             vmem_limit_bytes=min(est + VMEM_MARGIN_BYTES,
                                                                  VMEM_CAP_BYTES - VMEM_MARGIN_BYTES // 2)),
        name=name,
    )(*args)


def _lru_kernel(x_ref, y_ref, cw_ref, cb_ref, wa_ref, ba_ref, wx_ref, bx_ref, lam_ref, gain_ref,
                o_ref, xext_ref, hcar_ref):
    t, width = x_ref.shape
    nblk = width // HEAD_DIM

    @pl.when(pl.program_id(1) == 0)
    def _():
        xext_ref[0:SUBLANES, :] = jnp.zeros((SUBLANES, width), F32)
        hcar_ref[...] = jnp.zeros_like(hcar_ref)

    xext_ref[SUBLANES:SUBLANES + t, :] = x_ref[...]
    cw = cw_ref[...]
    kw = cw.shape[0]
    xc = cb_ref[...]
    for s in range(kw):
        xc = xc + cw[kw - 1 - s:kw - s, :] * xext_ref[SUBLANES - s:SUBLANES - s + t, :]
    xext_ref[0:SUBLANES, :] = xext_ref[t:t + SUBLANES, :]

    xcb = xc.astype(BF16)
    ra, rx = [], []
    for n in range(nblk):
        xs = xcb[:, n * HEAD_DIM:(n + 1) * HEAD_DIM]
        ra.append(jnp.dot(xs, wa_ref[n].astype(BF16), preferred_element_type=F32))
        rx.append(jnp.dot(xs, wx_ref[n].astype(BF16), preferred_element_type=F32))
    r = _sigmoid(jnp.concatenate(ra, axis=1) + ba_ref[...])
    gi = _sigmoid(jnp.concatenate(rx, axis=1) + bx_ref[...])
    log_a = (-LRU_C) * r * _softplus(-lam_ref[...])
    a = jnp.exp(log_a)
    u = jnp.sqrt(-jnp.tanh(log_a) * (a * a + 1.0)) * (gi * xc)

    ngroups = t // SUBLANES
    u = u.reshape(ngroups, SUBLANES, width)
    a = a.reshape(ngroups, SUBLANES, width)
    pos = lax.broadcasted_iota(jnp.int32, (1, SUBLANES, 1), 1)
    sh = 1
    while sh < SUBLANES:
        inside = pos >= sh
        u = u + a * jnp.where(inside, pltpu.roll(u, sh, 1), 0.0)
        a = a * jnp.where(inside, pltpu.roll(a, sh, 1), 1.0)
        sh *= 2
    h_prev = hcar_ref[0:1, :]
    groups = []
    for g in range(ngroups):
        hg = u[g] + a[g] * h_prev
        groups.append(hg)
        h_prev = hg[SUBLANES - 1:SUBLANES, :]
    h = jnp.concatenate(groups, axis=0)
    hcar_ref[...] = jnp.broadcast_to(h_prev, hcar_ref.shape)

    gate = jax.nn.gelu(y_ref[...])
    gain = gain_ref[...]
    for n in range(nblk):
        sl = slice(n * HEAD_DIM, (n + 1) * HEAD_DIM)
        o_ref[:, sl] = (_head_rms(h[:, sl], gain[:, sl]) * gate[:, sl]).astype(o_ref.dtype)


def _lru_group(proj, p, layer, batch, seq):
    width = p["lru_conv_w"].shape[-1]
    nblk = width // HEAD_DIM
    t = _pick(seq, (256, 128, 64, 8))
    nc = seq // t
    kw = p["lru_conv_w"].shape[1]
    row = lambda b, c: (b * nc + c, 0)
    vec = lambda b, c: (layer, 0, 0)
    est = 2 * 2 * t * width * 4 + 2 * t * width * 2 + 16 * t * width * 4
    return pl.pallas_call(
        _lru_kernel,
        grid=(batch, nc),
        in_specs=[pl.BlockSpec((t, width), row),
                  pl.BlockSpec((t, width), lambda b, c: (b * nc + c, 1)),
                  pl.BlockSpec((None, kw, width), vec),
                  pl.BlockSpec((None, 1, width), vec),
                  pl.BlockSpec((None, nblk, HEAD_DIM, HEAD_DIM), lambda b, c: (layer, 0, 0, 0)),
                  pl.BlockSpec((None, 1, width), vec),
                  pl.BlockSpec((None, nblk, HEAD_DIM, HEAD_DIM), lambda b, c: (layer, 0, 0, 0)),
                  pl.BlockSpec((None, 1, width), vec),
                  pl.BlockSpec((None, 1, width), vec),
                  pl.BlockSpec((None, 1, width), vec)],
        out_specs=pl.BlockSpec((t, width), row),
        out_shape=jax.ShapeDtypeStruct((batch * seq, width), BF16),
        scratch_shapes=[pltpu.VMEM((t + SUBLANES, width), F32), pltpu.VMEM((SUBLANES, width), F32)],
        compiler_params=_params(("arbitrary", "arbitrary"), est),
        name="rg_lru",
    )(proj, proj, p["lru_conv_w"], p["lru_conv_b"], p["lru_gate_a_w"], p["lru_gate_a_b"],
      p["lru_gate_x_w"], p["lru_gate_x_b"], p["lru_lambda"], p["mix_norm_w"])


def _block_mid_rows(x, m, rows):
    t, w = x.shape
    if m >= SUBLANES:
        return jnp.concatenate(
            [jnp.broadcast_to(x[b * 2 * m + m - 1:b * 2 * m + m, :], (2 * m, w)) for b in range(t // (2 * m))], axis=0)
    x3 = x.reshape(t // SUBLANES, SUBLANES, w)

    def pick(i):
        return jnp.broadcast_to(x3[:, i:i + 1, :], x3.shape).reshape(t, w)

    pos = rows & (SUBLANES - 1)
    out = pick(m - 1)
    for b in range(1, SUBLANES // (2 * m)):
        out = jnp.where(pos >= b * 2 * m, pick(b * 2 * m + m - 1), out)
    return out


def _hgrn_kernel(q_ref, f_ref, i_ref, g_ref, lbl_ref, gain_ref, o_ref, state_ref, *, layer):
    t, width = q_ref.shape
    nheads = width // HEAD_DIM

    @pl.when(pl.program_id(1) == 0)
    def _():
        state_ref[...] = jnp.zeros_like(state_ref)

    lbl = lbl_ref[...]
    pe = jnp.exp(lbl - jnp.max(lbl, axis=0, keepdims=True))
    psm = pe / jnp.sum(pe, axis=0, keepdims=True)
    lb = jnp.zeros((1, width), F32)
    for d in range(1, layer + 1):
        lb = lb + psm[d:d + 1, :]

    z = f_ref[...]
    la = jnp.log(lb)
    lbz = jnp.log1p(-lb) + _log_sigmoid(z)
    mx = jnp.maximum(la, lbz)
    log_f = mx + jnp.log(jnp.exp(la - mx) + jnp.exp(lbz - mx))
    k_all = (1.0 - lb) * _sigmoid(-z)
    qz = q_ref[...]
    q_all = qz * _sigmoid(qz)
    v_all = i_ref[...]
    gate = _sigmoid(g_ref[...])
    gain = gain_ref[...]

    rows = _row_iota((t, 1))
    rr = _row_iota((t, t))
    cc = lax.broadcasted_iota(jnp.int32, (t, t), 1)

    cum_all = _cumsum_rows(log_f)

    for h in range(nheads):
        sl = slice(h * HEAD_DIM, (h + 1) * HEAD_DIM)
        q, k, v = q_all[:, sl], k_all[:, sl], v_all[:, sl]
        cum = cum_all[:, sl]
        last = cum[t - 1:t, :]
        st = state_ref[h]
        o = lax.dot_general((q * jnp.exp(cum)).astype(BF16), st.astype(BF16), NT_DIMS,
                            preferred_element_type=F32)

        s_mat = jnp.where(rr == cc, jnp.sum(q * k, axis=-1, keepdims=True), 0.0)
        m = 1
        while 2 * m <= t:
            cref = _block_mid_rows(cum, m, rows)
            e = jnp.exp(-jnp.abs(cum - cref))
            qrole = (rows & (2 * m - 1)) >= m
            qp = jnp.where(qrole, q * e, 0.0).astype(BF16)
            kp = jnp.where(qrole, 0.0, k * e).astype(BF16)
            s_l = lax.dot_general(qp, kp, NT_DIMS, preferred_element_type=F32)
            s_mat = s_mat + jnp.where((rr ^ cc) < 2 * m, s_l, 0.0)
            m *= 2
        o = o + jnp.dot(s_mat.astype(BF16), v.astype(BF16), preferred_element_type=F32)

        kd = (k * jnp.exp(last - cum)).astype(BF16)
        state_ref[h] = st * jnp.exp(last) + jnp.dot(v.T.astype(BF16), kd, preferred_element_type=F32)

        o_ref[:, sl] = (_head_rms(o, gain[:, sl]) * gate[:, sl]).astype(o_ref.dtype)


def _hgrn_group(proj, p, layer, batch, seq, col0):
    depth, width = p["hgrn_lb_logits"].shape
    nheads = width // HEAD_DIM
    t = _pick(seq, (128, 64, 32, 16))
    nc = seq // t
    blk = lambda k: pl.BlockSpec((t, width), lambda b, c, k=k: (b * nc + c, col0 + k))
    gcol = p["lru_conv_w"].shape[-1] // width
    est = 2 * 4 * t * width * 4 + 2 * t * width * 2 + nheads * HEAD_DIM * HEAD_DIM * 4 + 24 * t * width * 4
    return pl.pallas_call(
        functools.partial(_hgrn_kernel, layer=layer),
        grid=(batch, nc),
        in_specs=[blk(0), blk(1), blk(2), blk(3),
                  pl.BlockSpec((depth, width), lambda b, c: (0, 0)),
                  pl.BlockSpec((None, 1, width), lambda b, c: (layer, 0, gcol))],
        out_specs=pl.BlockSpec((t, width), lambda b, c: (b * nc + c, 0)),
        out_shape=jax.ShapeDtypeStruct((batch * seq, width), BF16),
        scratch_shapes=[pltpu.VMEM((nheads, HEAD_DIM, HEAD_DIM), F32)],
        compiler_params=_params(("arbitrary", "arbitrary"), est),
        name="hgrn2",
    )(proj, proj, proj, proj, p["hgrn_lb_logits"], p["mix_norm_w"])


def _fox_cum_kernel(f_ref, b_ref, cum_ref, cumt_ref):
    s = f_ref.shape[0]
    cum = _log_sigmoid(f_ref[...] + b_ref[...])
    sh = 1
    while sh < s:
        cum = cum + _shift_rows(cum, sh, 0.0)
        sh *= 2
    cum = cum * LOG2_E
    cum_ref[...] = cum
    cumt_ref[...] = cum.T


def _fox_cum(flog, bias_row, batch, seq):
    est = 2 * 3 * seq * LANES * 4 + 8 * seq * LANES * 4
    return pl.pallas_call(
        _fox_cum_kernel,
        grid=(batch,),
        in_specs=[pl.BlockSpec((seq, LANES), lambda b: (b, 0)),
                  pl.BlockSpec((1, LANES), lambda b: (0, 0))],
        out_specs=[pl.BlockSpec((seq, LANES), lambda b: (b, 0)),
                   pl.BlockSpec((None, LANES, seq), lambda b: (b, 0, 0))],
        out_shape=[jax.ShapeDtypeStruct((batch * seq, LANES), F32),
                   jax.ShapeDtypeStruct((batch, LANES, seq), F32)],
        compiler_params=_params(("arbitrary",), est),
        name="fox_cum",
    )(flog, bias_row)


def _fox_kernel(q_ref, k_ref, v_ref, cq_ref, ck_ref, gain_ref, o_ref, kbf_ref, vbf_ref, s_ref, m_ref, acc_ref):
    tq = q_ref.shape[0]
    head = pl.program_id(1)
    qi = pl.program_id(2)

    @pl.when(qi == 0)
    def _():
        _cast_weight(k_ref, kbf_ref, tq)

        def body(c, carry):
            r = pl.multiple_of(c * tq, tq)
            vbf_ref[pl.ds(r, tq), 0:HEAD_DIM] = v_ref[pl.ds(r, tq), :].astype(BF16)
            vbf_ref[pl.ds(r, tq), HEAD_DIM:2 * HEAD_DIM] = jnp.ones((tq, HEAD_DIM), BF16)
            return carry
        lax.fori_loop(0, v_ref.shape[0] // tq, body, 0)

    q = (q_ref[...] * (HEAD_DIM ** -0.5 * LOG2_E)).astype(BF16)
    lane = lax.broadcasted_iota(jnp.int32, (1, LANES), 1)
    cq = jnp.sum(jnp.where(lane == head, cq_ref[...], 0.0), axis=-1, keepdims=True)

    m_ref[...] = jnp.full_like(m_ref, NEG_BIG)

    def lane_tile_max(s):
        part = s[:, 0:LANES]
        for c in range(1, s.shape[1] // LANES):
            part = jnp.maximum(part, s[:, c * LANES:(c + 1) * LANES])
        return part

    def logits(ki, masked):
        r0 = pl.multiple_of(ki * tq, tq)
        s = lax.dot_general(q, kbf_ref[pl.ds(r0, tq), :], NT_DIMS, preferred_element_type=F32)
        s = (s + cq) - ck_ref[pl.ds(ki, 1), :]
        if masked:
            s = jnp.where(_row_iota((tq, tq)) >= lax.broadcasted_iota(jnp.int32, (tq, tq), 1), s, NEG_BIG)
        s_ref[ki] = s
        m_ref[...] = jnp.maximum(m_ref[...], lane_tile_max(s))

    def pass1(ki, carry):
        logits(ki, False)
        return carry

    lax.fori_loop(0, qi, pass1, 0)
    logits(qi, True)

    m = jnp.max(m_ref[...], axis=-1, keepdims=True)
    acc_ref[...] = jnp.zeros_like(acc_ref)

    def pass2(ki, carry):
        r0 = pl.multiple_of(ki * tq, tq)
        pr = jnp.exp2(s_ref[ki] - m).astype(BF16)
        acc_ref[...] += jnp.dot(pr, vbf_ref[pl.ds(r0, tq), :], preferred_element_type=F32)
        return carry

    lax.fori_loop(0, qi + 1, pass2, 0)

    acc = acc_ref[...]
    o = acc[:, 0:HEAD_DIM] / acc[:, HEAD_DIM:HEAD_DIM + 1]
    o_ref[...] = _head_rms(o, gain_ref[...]).astype(o_ref.dtype)


def _fox_group(proj, cum, cumt, p, layer, batch, seq, qcol0, gcol0):
    nheads = p["fox_f_bias"].shape[-1]
    tq = _pick(seq, (1024, 512, 256, 128))
    nq = seq // tq
    cumt4 = cumt.reshape(batch, LANES, nq, tq)
    est = (2 * tq * HEAD_DIM * 4 + 2 * 2 * seq * HEAD_DIM * 4 + 3 * seq * HEAD_DIM * 2 + 2 * tq * LANES * 4
           + 2 * nq * tq * 4 + 2 * tq * HEAD_DIM * 2 + 3 * tq * LANES * 4 + (nq + 4) * tq * tq * 4)
    return pl.pallas_call(
        _fox_kernel,
        grid=(batch, nheads, nq),
        in_specs=[pl.BlockSpec((tq, HEAD_DIM), lambda b, h, i: (b * nq + i, qcol0 + h)),
                  pl.BlockSpec((seq, HEAD_DIM), lambda b, h, i: (b, qcol0 + nheads + h)),
                  pl.BlockSpec((seq, HEAD_DIM), lambda b, h, i: (b, qcol0 + 2 * nheads + h)),
                  pl.BlockSpec((tq, LANES), lambda b, h, i: (b * nq + i, 0)),
                  pl.BlockSpec((None, None, nq, tq), lambda b, h, i: (b, h, 0, 0)),
                  pl.BlockSpec((None, 1, HEAD_DIM), lambda b, h, i: (layer, 0, gcol0 + h))],
        out_specs=pl.BlockSpec((tq, HEAD_DIM), lambda b, h, i: (b * nq + i, h)),
        out_shape=jax.ShapeDtypeStruct((batch * seq, nheads * HEAD_DIM), BF16),
        scratch_shapes=[pltpu.VMEM((seq, HEAD_DIM), BF16), pltpu.VMEM((seq, 2 * HEAD_DIM), BF16),
                        pltpu.VMEM((nq, tq, tq), F32), pltpu.VMEM((tq, LANES), F32),
                        pltpu.VMEM((tq, 2 * HEAD_DIM), F32)],
        compiler_params=_params(("arbitrary", "arbitrary", "arbitrary"), est),
        name="fox_attention",
    )(proj, proj, proj, cum, cumt4, p["mix_norm_w"])


def _ffn_up_kernel(a_ref, wg_ref, wv_ref, cwg_ref, cwv_ref, cbg_ref, cbv_ref, o_ref,
                   wbf_ref, u_ref, *, tiles_per_seq, cast_chunk, row_chunk):
    i = pl.program_id(1)
    bm = a_ref.shape[0]
    bn = wg_ref.shape[1]

    @pl.when(i == 0)
    def _():
        _cast_weight(wg_ref, wbf_ref, cast_chunk)
        _cast_weight(wv_ref, wbf_ref, cast_chunk, col0=bn)

    @pl.when(i % tiles_per_seq == 0)
    def _():
        u_ref[0:SUBLANES, :] = jnp.zeros((SUBLANES, 2 * bn), F32)

    cw = jnp.concatenate([cwg_ref[...], cwv_ref[...]], axis=1)
    cb = jnp.concatenate([cbg_ref[...], cbv_ref[...]], axis=1)
    kw = cw.shape[0]
    for r0 in range(0, bm, row_chunk):
        rows = slice(r0, r0 + row_chunk)
        u_ref[SUBLANES + r0:SUBLANES + r0 + row_chunk, :] = jnp.dot(a_ref[rows, :], wbf_ref[...],
                                                                    preferred_element_type=F32)
        y = cb
        for s in range(kw):
            start = SUBLANES + r0 - s
            y = y + cw[kw - 1 - s:kw - s, :] * u_ref[start:start + row_chunk, :]
        yg, yv = y[:, 0:bn], y[:, bn:2 * bn]
        o_ref[rows, :] = (yg * _sigmoid(yg) * yv).astype(o_ref.dtype)
    u_ref[0:SUBLANES, :] = u_ref[bm:bm + SUBLANES, :]


def _ffn_up(hn, p, layer, seq):
    n, d = hn.shape
    d_ff = p["ffn_w_down"].shape[1]
    kw = p["ffn_conv_w"].shape[1]
    bn = _pick(d_ff, (256, 128))
    bm = _pick(seq, (1024, 512, 256))
    nj = d_ff // bn
    est = (2 * bm * d * 2 + 2 * 2 * d * bn * 4 + 2 * d * bn * 2 + 2 * bm * bn * 2 + (bm + SUBLANES) * 2 * bn * 4
           + 6 * bm * bn * 4)
    kern = functools.partial(_ffn_up_kernel, tiles_per_seq=seq // bm, cast_chunk=_pick(d, (256, 128, 64, 8)),
                             row_chunk=min(bm, FFN_ROW_CHUNK))
    return pl.pallas_call(
        kern,
        grid=(nj, n // bm),
        in_specs=[pl.BlockSpec((bm, d), lambda j, i: (i, 0)),
                  pl.BlockSpec((None, d, bn), lambda j, i: (layer, 0, j)),
                  pl.BlockSpec((None, d, bn), lambda j, i: (layer, 0, j + nj)),
                  pl.BlockSpec((None, kw, bn), lambda j, i: (layer, 0, j)),
                  pl.BlockSpec((None, kw, bn), lambda j, i: (layer, 0, j + nj)),
                  pl.BlockSpec((None, 1, bn), lambda j, i: (layer, 0, j)),
                  pl.BlockSpec((None, 1, bn), lambda j, i: (layer, 0, j + nj))],
        out_specs=pl.BlockSpec((bm, bn), lambda j, i: (i, j)),
        out_shape=jax.ShapeDtypeStruct((n, d_ff), BF16),
        scratch_shapes=[pltpu.VMEM((d, 2 * bn), BF16), pltpu.VMEM((bm + SUBLANES, 2 * bn), F32)],
        compiler_params=pltpu.CompilerParams(dimension_semantics=("arbitrary", "arbitrary"),
                                             vmem_limit_bytes=min(est + VMEM_MARGIN_BYTES,
                                                                  VMEM_CAP_BYTES - VMEM_MARGIN_BYTES // 2)),
        name="ffn_up_conv_gate",
    )(hn, p["ffn_w_up"], p["ffn_w_up"], p["ffn_conv_w"], p["ffn_conv_w"], p["ffn_conv_b"], p["ffn_conv_b"])


def kernel(x, ln1_w, w_in, lru_conv_w, lru_conv_b, lru_gate_a_w, lru_gate_a_b, lru_gate_x_w, lru_gate_x_b,
           lru_lambda, hgrn_lb_logits, fox_f_bias, mix_norm_w, w_out, ln2_w, ffn_w_up, ffn_conv_w, ffn_conv_b,
           ffn_w_down, final_norm_w):
    batch, seq, d_model = x.shape
    depth = w_in.shape[0]
    n = batch * seq
    lru_w = lru_conv_w.shape[-1]
    hg_w = hgrn_lb_logits.shape[-1]
    fox_heads = fox_f_bias.shape[-1]
    fox_w = fox_heads * HEAD_DIM
    d_main = 2 * lru_w + 4 * hg_w + 3 * fox_w
    d_mix = lru_w + hg_w + fox_w
    d_ff = ffn_w_down.shape[1]
    assert w_in.shape[-1] == d_main + fox_heads and fox_heads <= LANES
    assert lru_w == hg_w, "column-block addressing of the projection assumes equal LRU / HGRN2 widths"

    row3 = lambda a: a.reshape(a.shape[0], 1, a.shape[-1])
    p = {
        "lru_conv_w": lru_conv_w, "lru_conv_b": row3(lru_conv_b),
        "lru_gate_a_w": lru_gate_a_w, "lru_gate_a_b": row3(lru_gate_a_b),
        "lru_gate_x_w": lru_gate_x_w, "lru_gate_x_b": row3(lru_gate_x_b),
        "lru_lambda": row3(lru_lambda), "hgrn_lb_logits": hgrn_lb_logits, "fox_f_bias": fox_f_bias,
        "mix_norm_w": row3(mix_norm_w), "ffn_w_up": ffn_w_up, "ffn_conv_w": ffn_conv_w,
        "ffn_conv_b": row3(ffn_conv_b), "ffn_w_down": ffn_w_down,
    }
    ln1 = row3(ln1_w)
    ln2 = row3(ln2_w)
    w_in_t = jnp.swapaxes(w_in, 1, 2)
    w_f_t = jnp.pad(w_in_t[:, d_main:, :], ((0, 0), (0, LANES - fox_heads), (0, 0)))
    f_bias = jnp.pad(fox_f_bias, ((0, 0), (0, LANES - fox_heads)))

    bm = _pick(n, (512, 256, 128))
    bm_big = _pick(n, (1024, 512, 256, 128))
    bn_in = _pick(d_main, (1024, 512, 256, 128))
    bn_out = _pick(d_model, (1024, 512, 256, 128))
    bn_down = _pick(d_model, (512, 256, 128))
    k_half = d_ff // 2
    assert k_half % LANES == 0

    xf = x.reshape(n, d_model)
    for l in range(depth):
        hn = _rmsnorm(xf, ln1, l, BF16)
        proj = _matmul_stream([hn], w_in_t, w_lead=(l,), k_block=0, k_size=d_model, n_cols=d_main, bn=bn_in,
                              bm=bm_big, w_transposed=True, name="in_proj")
        flog = _matmul([hn], w_f_t, w_lead=(l,), k_block=0, k_size=d_model, n_cols=LANES, bn=LANES, bm=bm_big,
                       w_transposed=True, name="fox_forget_proj")
        cum, cumt = _fox_cum(flog, f_bias[l:l + 1], batch, seq)

        out_lru = _lru_group(proj, p, l, batch, seq)
        out_hg = _hgrn_group(proj, p, l, batch, seq, col0=2 * lru_w // hg_w)
        out_fox = _fox_group(proj, cum, cumt, p, l, batch, seq,
                             qcol0=(2 * lru_w + 4 * hg_w) // HEAD_DIM, gcol0=(lru_w + hg_w) // HEAD_DIM)

        xf = _matmul_stream([out_lru, out_hg, out_fox], w_out, w_lead=(l,), k_block=0, k_size=d_mix,
                            n_cols=d_model, bn=bn_out, bm=bm, res=xf, name="out_proj")

        hn2 = _rmsnorm(xf, ln2, l, BF16)
        hmid = _ffn_up(hn2, p, l, seq)
        for kb in range(2):
            xf = _matmul_stream([hmid], ffn_w_down, w_lead=(l,), k_block=kb, k_size=k_half, n_cols=d_model,
                                bn=bn_down, bm=bm_big, res=xf, name=f"ffn_down_{kb}")

    out = _rmsnorm(xf, final_norm_w.reshape(1, 1, d_model), 0, x.dtype)
    return out.reshape(batch, seq, d_model)
```

```python
import functools

import jax
import jax.numpy as jnp
from jax import lax
from jax.experimental import pallas as pl
from jax.experimental.pallas import tpu as pltpu

F32 = jnp.float32
BF16 = jnp.bfloat16

HEAD_DIM = 128
LRU_C = 8.0
EPS = 1e-6
NEG_BIG = -1e30
LOG2_E = 1.4426950408889634

SUBLANES = 8
LANES = 128
VMEM_CAP_BYTES = 64 * 1024 * 1024
VMEM_MARGIN_BYTES = 6 * 1024 * 1024

NT_DIMS = (((1,), (1,)), ((), ()))
FFN_ROW_CHUNK = 512
FFN_TAIL_CHUNK = 256


def _ffn_row_chunks(bm):
    if bm < FFN_ROW_CHUNK + 2 * FFN_TAIL_CHUNK:
        return (bm,) if bm <= FFN_TAIL_CHUNK else (bm - FFN_TAIL_CHUNK, FFN_TAIL_CHUNK)
    n_big = (bm - 2 * FFN_TAIL_CHUNK) // FFN_ROW_CHUNK
    rest = bm - n_big * FFN_ROW_CHUNK
    return (FFN_ROW_CHUNK,) * n_big + (rest - FFN_TAIL_CHUNK, FFN_TAIL_CHUNK)


def _vmem_limit(estimate_bytes):
    return int(min(max(2 * estimate_bytes, 16 * 1024 * 1024), VMEM_CAP_BYTES - VMEM_MARGIN_BYTES))


def _params(semantics, vmem_estimate):
    return pltpu.CompilerParams(dimension_semantics=semantics, vmem_limit_bytes=_vmem_limit(vmem_estimate))


def _pick(n, candidates):
    for c in candidates:
        if n % c == 0:
            return c
    raise ValueError(f"no tile in {candidates} divides {n}")


def _row_iota(shape):
    return lax.broadcasted_iota(jnp.int32, shape, 0)


def _shift_rows(x, shift, fill):
    rolled = pltpu.roll(x, shift, 0)
    return jnp.where(_row_iota(x.shape) >= shift, rolled, fill)


def _cumsum_rows(x):
    t, w = x.shape
    ngroups = t // SUBLANES
    x3 = x.reshape(ngroups, SUBLANES, w)
    pos = lax.broadcasted_iota(jnp.int32, (1, SUBLANES, 1), 1)
    sh = 1
    while sh < SUBLANES:
        x3 = x3 + jnp.where(pos >= sh, pltpu.roll(x3, sh, 1), 0.0)
        sh *= 2
    groups = [x3[0]]
    for g in range(1, ngroups):
        groups.append(x3[g] + groups[-1][SUBLANES - 1:SUBLANES, :])
    return jnp.concatenate(groups, axis=0)


def _sigmoid(x):
    return 0.5 * jnp.tanh(0.5 * x) + 0.5


def _log_sigmoid(x):
    return jnp.minimum(x, 0.0) - jnp.log1p(jnp.exp(-jnp.abs(x)))


def _softplus(x):
    return jnp.maximum(x, 0.0) + jnp.log1p(jnp.exp(-jnp.abs(x)))


def _head_rms(o, gain):
    var = jnp.mean(o * o, axis=-1, keepdims=True)
    return o * lax.rsqrt(var + EPS) * gain


def _rmsnorm_kernel(x_ref, g_ref, o_ref):
    x = x_ref[...]
    var = jnp.mean(x * x, axis=-1, keepdims=True)
    o_ref[...] = (x * lax.rsqrt(var + EPS) * g_ref[...]).astype(o_ref.dtype)


def _rmsnorm(x, gains, layer, out_dtype):
    n, d = x.shape
    rows = _pick(n, (512, 256, 128, 64, 8))
    est = 2 * rows * d * (4 + jnp.dtype(out_dtype).itemsize) + 2 * rows * d * 4
    return pl.pallas_call(
        _rmsnorm_kernel,
        grid=(n // rows,),
        in_specs=[pl.BlockSpec((rows, d), lambda i: (i, 0)),
                  pl.BlockSpec((None, 1, d), lambda i: (layer, 0, 0))],
        out_specs=pl.BlockSpec((rows, d), lambda i: (i, 0)),
        out_shape=jax.ShapeDtypeStruct((n, d), out_dtype),
        compiler_params=_params(("arbitrary",), est),
        name="rmsnorm",
    )(x, gains)


def _cast_weight(w_ref, wbf_ref, chunk, col0=0):
    cols = w_ref.shape[1]

    def body(c, carry):
        r = pl.multiple_of(c * chunk, chunk)
        wbf_ref[pl.ds(r, chunk), col0:col0 + cols] = w_ref[pl.ds(r, chunk), :].astype(BF16)
        return carry
    lax.fori_loop(0, w_ref.shape[0] // chunk, body, 0)


def _cast_weight_transposed(wt_ref, wbf_ref, chunk):
    for c in range(wt_ref.shape[1] // chunk):
        wbf_ref[c * chunk:(c + 1) * chunk, :] = wt_ref[:, c * chunk:(c + 1) * chunk].T.astype(BF16)


def _mm_kernel(*refs, a_segments, has_res, cast_chunk, w_transposed):
    n_a = len(a_segments)
    a_refs = refs[:n_a]
    w_ref = refs[n_a]
    res_ref = refs[n_a + 1] if has_res else None
    o_ref = refs[n_a + 1 + has_res]
    wbf_ref = refs[n_a + 2 + has_res]

    @pl.when(pl.program_id(1) == 0)
    def _():
        if w_transposed:
            _cast_weight_transposed(w_ref, wbf_ref, cast_chunk)
        else:
            _cast_weight(w_ref, wbf_ref, cast_chunk)

    acc = None
    for a_ref, (off, width) in zip(a_refs, a_segments):
        part = jnp.dot(a_ref[...], wbf_ref[off:off + width, :], preferred_element_type=F32)
        acc = part if acc is None else acc + part
    if has_res:
        acc = acc + res_ref[...]
    o_ref[...] = acc.astype(o_ref.dtype)


def _matmul(a_list, w, *, w_lead, k_block, k_size, n_cols, col_block0=0, bn, bm, res=None,
            out_dtype=F32, w_transposed=False, name):
    n = a_list[0].shape[0]
    segs, off = [], 0
    for a in a_list:
        width = k_size if len(a_list) == 1 else a.shape[1]
        segs.append((off, width))
        off += width
    assert off == k_size
    lead = tuple(w_lead)
    if w_transposed:
        w_spec = pl.BlockSpec((None,) * len(lead) + (bn, k_size), lambda j, i: lead + (j + col_block0, k_block))
    else:
        w_spec = pl.BlockSpec((None,) * len(lead) + (k_size, bn), lambda j, i: lead + (k_block, j + col_block0))
    in_specs = [pl.BlockSpec((bm, width), lambda j, i, kb=(k_block if len(a_list) == 1 else 0): (i, kb))
                for (_, width) in segs]
    in_specs.append(w_spec)
    args = list(a_list) + [w]
    if res is not None:
        in_specs.append(pl.BlockSpec((bm, bn), lambda j, i: (i, j)))
        args.append(res)
    out_bytes = jnp.dtype(out_dtype).itemsize
    est = (2 * k_size * bn * 4 + k_size * bn * 2 + 2 * bm * k_size * 2 + 2 * bm * bn * out_bytes
           + (2 * bm * bn * 4 if res is not None else 0) + bm * bn * 4)
    kern = functools.partial(_mm_kernel, a_segments=tuple(segs), has_res=res is not None,
                             cast_chunk=_pick(k_size, (256, 128) if w_transposed else (256, 128, 64, 8)),
                             w_transposed=w_transposed)
    return pl.pallas_call(
        kern,
        grid=(n_cols // bn, n // bm),
        in_specs=in_specs,
        out_specs=pl.BlockSpec((bm, bn), lambda j, i: (i, j)),
        out_shape=jax.ShapeDtypeStruct((n, n_cols), out_dtype),
        scratch_shapes=[pltpu.VMEM((k_size, bn), BF16)],
        compiler_params=pltpu.CompilerParams(dimension_semantics=("arbitrary", "arbitrary"),
                                             vmem_limit_bytes=min(est + VMEM_MARGIN_BYTES,
                                                                  VMEM_CAP_BYTES - VMEM_MARGIN_BYTES // 2)),
        name=name,
    )(*args)


def _mm_stream_kernel(*refs, a_segments, has_res, w_transposed):
    n_a = len(a_segments)
    a_refs = refs[:n_a]
    w_ref = refs[n_a]
    res_ref = refs[n_a + 1] if has_res else None
    o_ref = refs[n_a + 1 + has_res]
    wbf_refs = refs[n_a + 2 + has_res:n_a + 4 + has_res]
    jj = pl.program_id(0)
    ck = w_ref.shape[1] if w_transposed else w_ref.shape[0]
    r = pl.multiple_of(pl.program_id(1) * ck, ck)

    def stage(dst_ref):
        chunk = w_ref[...].T if w_transposed else w_ref[...]
        dst_ref[pl.ds(r, ck), :] = chunk.astype(BF16)

    def compute(src_ref):
        acc = None
        for a_ref, (off, width) in zip(a_refs, a_segments):
            part = jnp.dot(a_ref[...], src_ref[off:off + width, :], preferred_element_type=F32)
            acc = part if acc is None else acc + part
        if has_res:
            acc = acc + res_ref[...]
        o_ref[...] = acc.astype(o_ref.dtype)

    @pl.when(jj == 0)
    def _():
        stage(wbf_refs[0])

    for parity in range(2):
        @pl.when((jj > 0) & (jj % 2 == parity))
        def _(parity=parity):
            stage(wbf_refs[parity])
            compute(wbf_refs[1 - parity])


def _matmul_stream(a_list, w, *, w_lead, k_block, k_size, n_cols, col_block0=0, bn, bm, res=None,
                   out_dtype=F32, w_transposed=False, name):
    n = a_list[0].shape[0]
    segs, off = [], 0
    for a in a_list:
        width = k_size if len(a_list) == 1 else a.shape[1]
        segs.append((off, width))
        off += width
    assert off == k_size
    ni, nj = n // bm, n_cols // bn
    assert k_size % ni == 0
    ck = k_size // ni
    assert ck % (LANES if w_transposed else 2 * SUBLANES) == 0
    lead = tuple(w_lead)
    row = lambda jj, i: jnp.where(jj == 0, 0, i)
    col = lambda jj: jnp.maximum(jj - 1, 0)
    nxt = lambda jj: jnp.minimum(jj, nj - 1) + col_block0
    if w_transposed:
        w_spec = pl.BlockSpec((None,) * len(lead) + (bn, ck), lambda jj, i: lead + (nxt(jj), k_block * ni + i))
    else:
        w_spec = pl.BlockSpec((None,) * len(lead) + (ck, bn), lambda jj, i: lead + (k_block * ni + i, nxt(jj)))
    in_specs = [pl.BlockSpec((bm, width), lambda jj, i, kb=(k_block if len(a_list) == 1 else 0): (row(jj, i), kb))
                for (_, width) in segs]
    in_specs.append(w_spec)
    args = list(a_list) + [w]
    if res is not None:
        in_specs.append(pl.BlockSpec((bm, bn), lambda jj, i: (row(jj, i), col(jj))))
        args.append(res)
    out_bytes = jnp.dtype(out_dtype).itemsize
    est = (2 * k_size * bn * 2 + 3 * ck * bn * 4 + 2 * bm * k_size * 2 + 2 * bm * bn * out_bytes
           + (2 * bm * bn * 4 if res is not None else 0) + 2 * bm * bn * 4)
    kern = functools.partial(_mm_stream_kernel, a_segments=tuple(segs), has_res=res is not None,
                             w_transposed=w_transposed)
    return pl.pallas_call(
        kern,
        grid=(nj + 1, ni),
        in_specs=in_specs,
        out_specs=pl.BlockSpec((bm, bn), lambda jj, i: (row(jj, i), col(jj))),
        out_shape=jax.ShapeDtypeStruct((n, n_cols), out_dtype),
        scratch_shapes=[pltpu.VMEM((k_size, bn), BF16), pltpu.VMEM((k_size, bn), BF16)],
        compiler_params=pltpu.CompilerParams(dimension_semantics=("arbitrary", "arbitrary"),
                                             vmem_limit_bytes=min(est + VMEM_MARGIN_BYTES,
                                                                  VMEM_CAP_BYTES - VMEM_MARGIN_BYTES // 2)),
        name=name,
    )(*args)


def _lru_kernel(x_ref, y_ref, cw_ref, cb_ref, wa_ref, ba_ref, wx_ref, bx_ref, lam_ref, gain_ref,
                o_ref, xext_ref, hcar_ref):
    t, width = x_ref.shape
    nblk = width // HEAD_DIM

    @pl.when(pl.program_id(1) == 0)
    def _():
        xext_ref[0:SUBLANES, :] = jnp.zeros((SUBLANES, width), F32)
        hcar_ref[...] = jnp.zeros_like(hcar_ref)

    xext_ref[SUBLANES:SUBLANES + t, :] = x_ref[...]
    cw = cw_ref[...]
    kw = cw.shape[0]
    xc = cb_ref[...]
    for s in range(kw):
        xc = xc + cw[kw - 1 - s:kw - s, :] * xext_ref[SUBLANES - s:SUBLANES - s + t, :]
    xext_ref[0:SUBLANES, :] = xext_ref[t:t + SUBLANES, :]

    xcb = xc.astype(BF16)
    ra, rx = [], []
    for n in range(nblk):
        xs = xcb[:, n * HEAD_DIM:(n + 1) * HEAD_DIM]
        ra.append(jnp.dot(xs, wa_ref[n].astype(BF16), preferred_element_type=F32))
        rx.append(jnp.dot(xs, wx_ref[n].astype(BF16), preferred_element_type=F32))
    r = _sigmoid(jnp.concatenate(ra, axis=1) + ba_ref[...])
    gi = _sigmoid(jnp.concatenate(rx, axis=1) + bx_ref[...])
    log_a = (-LRU_C) * r * _softplus(-lam_ref[...])
    a = jnp.exp(log_a)
    u = jnp.sqrt(-jnp.tanh(log_a) * (a * a + 1.0)) * (gi * xc)

    ngroups = t // SUBLANES
    u = u.reshape(ngroups, SUBLANES, width)
    a = a.reshape(ngroups, SUBLANES, width)
    pos = lax.broadcasted_iota(jnp.int32, (1, SUBLANES, 1), 1)
    sh = 1
    while sh < SUBLANES:
        inside = pos >= sh
        u = u + a * jnp.where(inside, pltpu.roll(u, sh, 1), 0.0)
        a = a * jnp.where(inside, pltpu.roll(a, sh, 1), 1.0)
        sh *= 2
    h_prev = hcar_ref[0:1, :]
    groups = []
    for g in range(ngroups):
        hg = u[g] + a[g] * h_prev
        groups.append(hg)
        h_prev = hg[SUBLANES - 1:SUBLANES, :]
    h = jnp.concatenate(groups, axis=0)
    hcar_ref[...] = jnp.broadcast_to(h_prev, hcar_ref.shape)

    gate = jax.nn.gelu(y_ref[...])
    gain = gain_ref[...]
    for n in range(nblk):
        sl = slice(n * HEAD_DIM, (n + 1) * HEAD_DIM)
        o_ref[:, sl] = (_head_rms(h[:, sl], gain[:, sl]) * gate[:, sl]).astype(o_ref.dtype)


def _lru_group(proj, p, layer, batch, seq):
    width = p["lru_conv_w"].shape[-1]
    nblk = width // HEAD_DIM
    t = _pick(seq, (256, 128, 64, 8))
    nc = seq // t
    kw = p["lru_conv_w"].shape[1]
    row = lambda b, c: (b * nc + c, 0)
    vec = lambda b, c: (layer, 0, 0)
    est = 2 * 2 * t * width * 4 + 2 * t * width * 2 + 16 * t * width * 4
    return pl.pallas_call(
        _lru_kernel,
        grid=(batch, nc),
        in_specs=[pl.BlockSpec((t, width), row),
                  pl.BlockSpec((t, width), lambda b, c: (b * nc + c, 1)),
                  pl.BlockSpec((None, kw, width), vec),
                  pl.BlockSpec((None, 1, width), vec),
                  pl.BlockSpec((None, nblk, HEAD_DIM, HEAD_DIM), lambda b, c: (layer, 0, 0, 0)),
                  pl.BlockSpec((None, 1, width), vec),
                  pl.BlockSpec((None, nblk, HEAD_DIM, HEAD_DIM), lambda b, c: (layer, 0, 0, 0)),
                  pl.BlockSpec((None, 1, width), vec),
                  pl.BlockSpec((None, 1, width), vec),
                  pl.BlockSpec((None, 1, width), vec)],
        out_specs=pl.BlockSpec((t, width), row),
        out_shape=jax.ShapeDtypeStruct((batch * seq, width), BF16),
        scratch_shapes=[pltpu.VMEM((t + SUBLANES, width), F32), pltpu.VMEM((SUBLANES, width), F32)],
        compiler_params=_params(("arbitrary", "arbitrary"), est),
        name="rg_lru",
    )(proj, proj, p["lru_conv_w"], p["lru_conv_b"], p["lru_gate_a_w"], p["lru_gate_a_b"],
      p["lru_gate_x_w"], p["lru_gate_x_b"], p["lru_lambda"], p["mix_norm_w"])


def _block_mid_rows(x, m, rows):
    t, w = x.shape
    if m >= SUBLANES:
        return jnp.concatenate(
            [jnp.broadcast_to(x[b * 2 * m + m - 1:b * 2 * m + m, :], (2 * m, w)) for b in range(t // (2 * m))], axis=0)
    x3 = x.reshape(t // SUBLANES, SUBLANES, w)

    def pick(i):
        return jnp.broadcast_to(x3[:, i:i + 1, :], x3.shape).reshape(t, w)

    pos = rows & (SUBLANES - 1)
    out = pick(m - 1)
    for b in range(1, SUBLANES // (2 * m)):
        out = jnp.where(pos >= b * 2 * m, pick(b * 2 * m + m - 1), out)
    return out


def _hgrn_kernel(q_ref, f_ref, i_ref, g_ref, lbl_ref, gain_ref, o_ref, state_ref, *, layer):
    t, width = q_ref.shape
    nheads = width // HEAD_DIM

    @pl.when(pl.program_id(1) == 0)
    def _():
        state_ref[...] = jnp.zeros_like(state_ref)

    lbl = lbl_ref[...]
    pe = jnp.exp(lbl - jnp.max(lbl, axis=0, keepdims=True))
    psm = pe / jnp.sum(pe, axis=0, keepdims=True)
    lb = jnp.zeros((1, width), F32)
    for d in range(1, layer + 1):
        lb = lb + psm[d:d + 1, :]

    z = f_ref[...]
    la = jnp.log(lb)
    lbz = jnp.log1p(-lb) + _log_sigmoid(z)
    mx = jnp.maximum(la, lbz)
    log_f = mx + jnp.log(jnp.exp(la - mx) + jnp.exp(lbz - mx))
    k_all = (1.0 - lb) * _sigmoid(-z)
    qz = q_ref[...]
    q_all = qz * _sigmoid(qz)
    v_all = i_ref[...]
    gate = _sigmoid(g_ref[...])
    gain = gain_ref[...]

    rows = _row_iota((t, 1))
    rr = _row_iota((t, t))
    cc = lax.broadcasted_iota(jnp.int32, (t, t), 1)

    cum_all = _cumsum_rows(log_f)

    for h in range(nheads):
        sl = slice(h * HEAD_DIM, (h + 1) * HEAD_DIM)
        q, k, v = q_all[:, sl], k_all[:, sl], v_all[:, sl]
        cum = cum_all[:, sl]
        last = cum[t - 1:t, :]
        st = state_ref[h]
        o = lax.dot_general((q * jnp.exp(cum)).astype(BF16), st.astype(BF16), NT_DIMS,
                            preferred_element_type=F32)

        s_mat = jnp.where(rr == cc, jnp.sum(q * k, axis=-1, keepdims=True), 0.0)
        m = 1
        while 2 * m <= t:
            cref = _block_mid_rows(cum, m, rows)
            e = jnp.exp(-jnp.abs(cum - cref))
            s_l = lax.dot_general((q * e).astype(BF16), (k * e).astype(BF16), NT_DIMS, preferred_element_type=F32)
            keep = ((rr ^ cc) < 2 * m) & ((rr & (2 * m - 1)) >= m) & ((cc & (2 * m - 1)) < m)
            s_mat = s_mat + jnp.where(keep, s_l, 0.0)
            m *= 2
        o = o + jnp.dot(s_mat.astype(BF16), v.astype(BF16), preferred_element_type=F32)

        kd = (k * jnp.exp(last - cum)).astype(BF16)
        state_ref[h] = st * jnp.exp(last) + jnp.dot(v.T.astype(BF16), kd, preferred_element_type=F32)

        o_ref[:, sl] = (_head_rms(o, gain[:, sl]) * gate[:, sl]).astype(o_ref.dtype)


def _hgrn_group(proj, p, layer, batch, seq, col0):
    depth, width = p["hgrn_lb_logits"].shape
    nheads = width // HEAD_DIM
    t = _pick(seq, (128, 64, 32, 16))
    nc = seq // t
    blk = lambda k: pl.BlockSpec((t, width), lambda b, c, k=k: (b * nc + c, col0 + k))
    gcol = p["lru_conv_w"].shape[-1] // width
    est = 2 * 4 * t * width * 4 + 2 * t * width * 2 + nheads * HEAD_DIM * HEAD_DIM * 4 + 24 * t * width * 4
    return pl.pallas_call(
        functools.partial(_hgrn_kernel, layer=layer),
        grid=(batch, nc),
        in_specs=[blk(0), blk(1), blk(2), blk(3),
                  pl.BlockSpec((depth, width), lambda b, c: (0, 0)),
                  pl.BlockSpec((None, 1, width), lambda b, c: (layer, 0, gcol))],
        out_specs=pl.BlockSpec((t, width), lambda b, c: (b * nc + c, 0)),
        out_shape=jax.ShapeDtypeStruct((batch * seq, width), BF16),
        scratch_shapes=[pltpu.VMEM((nheads, HEAD_DIM, HEAD_DIM), F32)],
        compiler_params=_params(("arbitrary", "arbitrary"), est),
        name="hgrn2",
    )(proj, proj, proj, proj, p["hgrn_lb_logits"], p["mix_norm_w"])


def _fox_cum_kernel(f_ref, b_ref, cum_ref, cumt_ref):
    s = f_ref.shape[0]
    cum = _log_sigmoid(f_ref[...] + b_ref[...])
    sh = 1
    while sh < s:
        cum = cum + _shift_rows(cum, sh, 0.0)
        sh *= 2
    cum = cum * LOG2_E
    cum_ref[...] = cum
    cumt_ref[...] = cum.T


def _fox_cum(flog, bias_row, batch, seq):
    est = 2 * 3 * seq * LANES * 4 + 8 * seq * LANES * 4
    return pl.pallas_call(
        _fox_cum_kernel,
        grid=(batch,),
        in_specs=[pl.BlockSpec((seq, LANES), lambda b: (b, 0)),
                  pl.BlockSpec((1, LANES), lambda b: (0, 0))],
        out_specs=[pl.BlockSpec((seq, LANES), lambda b: (b, 0)),
                   pl.BlockSpec((None, LANES, seq), lambda b: (b, 0, 0))],
        out_shape=[jax.ShapeDtypeStruct((batch * seq, LANES), F32),
                   jax.ShapeDtypeStruct((batch, LANES, seq), F32)],
        compiler_params=_params(("arbitrary",), est),
        name="fox_cum",
    )(flog, bias_row)


def _fox_kernel(q_ref, k_ref, v_ref, cq_ref, ck_ref, gain_ref, o_ref, kbf_ref, vbf_ref, s_ref, m_ref, acc_ref):
    tq = q_ref.shape[0]
    head = pl.program_id(1)
    qi = pl.program_id(2)

    @pl.when(qi == 0)
    def _():
        _cast_weight(k_ref, kbf_ref, tq)

        def body(c, carry):
            r = pl.multiple_of(c * tq, tq)
            vbf_ref[pl.ds(r, tq), 0:HEAD_DIM] = v_ref[pl.ds(r, tq), :].astype(BF16)
            vbf_ref[pl.ds(r, tq), HEAD_DIM:2 * HEAD_DIM] = jnp.ones((tq, HEAD_DIM), BF16)
            return carry
        lax.fori_loop(0, v_ref.shape[0] // tq, body, 0)

    q = (q_ref[...] * (HEAD_DIM ** -0.5 * LOG2_E)).astype(BF16)
    lane = lax.broadcasted_iota(jnp.int32, (1, LANES), 1)
    cq = jnp.sum(jnp.where(lane == head, cq_ref[...], 0.0), axis=-1, keepdims=True)

    m_ref[...] = jnp.full_like(m_ref, NEG_BIG)

    def lane_tile_max(s):
        part = s[:, 0:LANES]
        for c in range(1, s.shape[1] // LANES):
            part = jnp.maximum(part, s[:, c * LANES:(c + 1) * LANES])
        return part

    def logits(ki, masked):
        r0 = pl.multiple_of(ki * tq, tq)
        s = lax.dot_general(q, kbf_ref[pl.ds(r0, tq), :], NT_DIMS, preferred_element_type=F32)
        s = (s + cq) - ck_ref[pl.ds(ki, 1), :]
        if masked:
            s = jnp.where(_row_iota((tq, tq)) >= lax.broadcasted_iota(jnp.int32, (tq, tq), 1), s, NEG_BIG)
        s_ref[ki] = s
        m_ref[...] = jnp.maximum(m_ref[...], lane_tile_max(s))

    def pass1(ki, carry):
        logits(ki, False)
        return carry

    lax.fori_loop(0, qi, pass1, 0)
    logits(qi, True)

    m = jnp.max(m_ref[...], axis=-1, keepdims=True)
    acc_ref[...] = jnp.zeros_like(acc_ref)

    def pass2(ki, carry):
        r0 = pl.multiple_of(ki * tq, tq)
        pr = jnp.exp2(s_ref[ki] - m).astype(BF16)
        acc_ref[...] += jnp.dot(pr, vbf_ref[pl.ds(r0, tq), :], preferred_element_type=F32)
        return carry

    lax.fori_loop(0, qi + 1, pass2, 0)

    acc = acc_ref[...]
    o = acc[:, 0:HEAD_DIM] / acc[:, HEAD_DIM:HEAD_DIM + 1]
    o_ref[...] = _head_rms(o, gain_ref[...]).astype(o_ref.dtype)


def _fox_group(proj, cum, cumt, p, layer, batch, seq, qcol0, gcol0):
    nheads = p["fox_f_bias"].shape[-1]
    tq = _pick(seq, (1024, 512, 256, 128))
    nq = seq // tq
    cumt4 = cumt.reshape(batch, LANES, nq, tq)
    est = (2 * tq * HEAD_DIM * 4 + 2 * 2 * seq * HEAD_DIM * 4 + 3 * seq * HEAD_DIM * 2 + 2 * tq * LANES * 4
           + 2 * nq * tq * 4 + 2 * tq * HEAD_DIM * 2 + 3 * tq * LANES * 4 + (nq + 4) * tq * tq * 4)
    return pl.pallas_call(
        _fox_kernel,
        grid=(batch, nheads, nq),
        in_specs=[pl.BlockSpec((tq, HEAD_DIM), lambda b, h, i: (b * nq + i, qcol0 + h)),
                  pl.BlockSpec((seq, HEAD_DIM), lambda b, h, i: (b, qcol0 + nheads + h)),
                  pl.BlockSpec((seq, HEAD_DIM), lambda b, h, i: (b, qcol0 + 2 * nheads + h)),
                  pl.BlockSpec((tq, LANES), lambda b, h, i: (b * nq + i, 0)),
                  pl.BlockSpec((None, None, nq, tq), lambda b, h, i: (b, h, 0, 0)),
                  pl.BlockSpec((None, 1, HEAD_DIM), lambda b, h, i: (layer, 0, gcol0 + h))],
        out_specs=pl.BlockSpec((tq, HEAD_DIM), lambda b, h, i: (b * nq + i, h)),
        out_shape=jax.ShapeDtypeStruct((batch * seq, nheads * HEAD_DIM), BF16),
        scratch_shapes=[pltpu.VMEM((seq, HEAD_DIM), BF16), pltpu.VMEM((seq, 2 * HEAD_DIM), BF16),
                        pltpu.VMEM((nq, tq, tq), F32), pltpu.VMEM((tq, LANES), F32),
                        pltpu.VMEM((tq, 2 * HEAD_DIM), F32)],
        compiler_params=_params(("arbitrary", "arbitrary", "arbitrary"), est),
        name="fox_attention",
    )(proj, proj, proj, cum, cumt4, p["mix_norm_w"])


def _ffn_up_kernel(a_ref, wg_ref, wv_ref, cwg_ref, cwv_ref, cbg_ref, cbv_ref, o_ref,
                   wa_ref, wb_ref, u_ref, *, tiles_per_seq, row_chunks):
    jj = pl.program_id(0)
    i = pl.program_id(1)
    bm = a_ref.shape[0]
    ck, bn = wg_ref.shape
    r = pl.multiple_of(i * ck, ck)

    def stage(dst_ref):
        dst_ref[pl.ds(r, ck), 0:bn] = wg_ref[...].astype(BF16)
        dst_ref[pl.ds(r, ck), bn:2 * bn] = wv_ref[...].astype(BF16)

    def compute(src_ref):
        cw = jnp.concatenate([cwg_ref[...], cwv_ref[...]], axis=1)
        cb = jnp.concatenate([cbg_ref[...], cbv_ref[...]], axis=1)
        kw = cw.shape[0]
        r0 = 0
        for nrows in row_chunks:
            rows = slice(r0, r0 + nrows)
            u_ref[SUBLANES + r0:SUBLANES + r0 + nrows, :] = jnp.dot(a_ref[rows, :], src_ref[...],
                                                                    preferred_element_type=F32)
            y = cb
            for s in range(kw):
                start = SUBLANES + r0 - s
                y = y + cw[kw - 1 - s:kw - s, :] * u_ref[start:start + nrows, :]
            yg, yv = y[:, 0:bn], y[:, bn:2 * bn]
            o_ref[rows, :] = (yg * _sigmoid(yg) * yv).astype(o_ref.dtype)
            r0 += nrows
        u_ref[0:SUBLANES, :] = u_ref[bm:bm + SUBLANES, :]

    @pl.when(jj == 0)
    def _():
        stage(wa_ref)

    @pl.when((jj > 0) & (i % tiles_per_seq == 0))
    def _():
        u_ref[0:SUBLANES, :] = jnp.zeros((SUBLANES, 2 * bn), F32)

    for parity, (dst_ref, src_ref) in enumerate(((wa_ref, wb_ref), (wb_ref, wa_ref))):
        @pl.when((jj > 0) & (jj % 2 == parity))
        def _(dst_ref=dst_ref, src_ref=src_ref):
            stage(dst_ref)
            compute(src_ref)


def _ffn_up(hn, p, layer, seq):
    n, d = hn.shape
    d_ff = p["ffn_w_down"].shape[1]
    kw = p["ffn_conv_w"].shape[1]
    bn = _pick(d_ff, (256, 128))
    bm = _pick(seq, (2048, 1024, 512, 256))
    nj, ni = d_ff // bn, n // bm
    assert d % ni == 0 and (d // ni) % (2 * SUBLANES) == 0
    ck = d // ni
    est = (2 * bm * d * 2 + 2 * d * 2 * bn * 2 + 2 * 2 * ck * bn * 4 + 2 * bm * bn * 2
           + (bm + SUBLANES) * 2 * bn * 4 + 8 * FFN_ROW_CHUNK * 2 * bn * 4)
    kern = functools.partial(_ffn_up_kernel, tiles_per_seq=seq // bm, row_chunks=_ffn_row_chunks(bm))
    row = lambda jj, i: jnp.where(jj == 0, 0, i)
    col = lambda jj: jnp.maximum(jj - 1, 0)
    nxt = lambda jj: jnp.minimum(jj, nj - 1)
    return pl.pallas_call(
        kern,
        grid=(nj + 1, ni),
        in_specs=[pl.BlockSpec((bm, d), lambda jj, i: (row(jj, i), 0)),
                  pl.BlockSpec((None, ck, bn), lambda jj, i: (layer, i, nxt(jj))),
                  pl.BlockSpec((None, ck, bn), lambda jj, i: (layer, i, nxt(jj) + nj)),
                  pl.BlockSpec((None, kw, bn), lambda jj, i: (layer, 0, col(jj))),
                  pl.BlockSpec((None, kw, bn), lambda jj, i: (layer, 0, col(jj) + nj)),
                  pl.BlockSpec((None, 1, bn), lambda jj, i: (layer, 0, col(jj))),
                  pl.BlockSpec((None, 1, bn), lambda jj, i: (layer, 0, col(jj) + nj))],
        out_specs=pl.BlockSpec((bm, bn), lambda jj, i: (row(jj, i), col(jj))),
        out_shape=jax.ShapeDtypeStruct((n, d_ff), BF16),
        scratch_shapes=[pltpu.VMEM((d, 2 * bn), BF16), pltpu.VMEM((d, 2 * bn), BF16),
                        pltpu.VMEM((bm + SUBLANES, 2 * bn), F32)],
        compiler_params=pltpu.CompilerParams(dimension_semantics=("arbitrary", "arbitrary"),
                                             vmem_limit_bytes=min(est + VMEM_MARGIN_BYTES,
                                                                  VMEM_CAP_BYTES - VMEM_MARGIN_BYTES // 2)),
        name="ffn_up_conv_gate",
    )(hn, p["ffn_w_up"], p["ffn_w_up"], p["ffn_conv_w"], p["ffn_conv_w"], p["ffn_conv_b"], p["ffn_conv_b"])


def kernel(x, ln1_w, w_in, lru_conv_w, lru_conv_b, lru_gate_a_w, lru_gate_a_b, lru_gate_x_w, lru_gate_x_b,
           lru_lambda, hgrn_lb_logits, fox_f_bias, mix_norm_w, w_out, ln2_w, ffn_w_up, ffn_conv_w, ffn_conv_b,
           ffn_w_down, final_norm_w):
    batch, seq, d_model = x.shape
    depth = w_in.shape[0]
    n = batch * seq
    lru_w = lru_conv_w.shape[-1]
    hg_w = hgrn_lb_logits.shape[-1]
    fox_heads = fox_f_bias.shape[-1]
    fox_w = fox_heads * HEAD_DIM
    d_main = 2 * lru_w + 4 * hg_w + 3 * fox_w
    d_mix = lru_w + hg_w + fox_w
    d_ff = ffn_w_down.shape[1]
    assert w_in.shape[-1] == d_main + fox_heads and fox_heads <= LANES
    assert lru_w == hg_w, "column-block addressing of the projection assumes equal LRU / HGRN2 widths"

    row3 = lambda a: a.reshape(a.shape[0], 1, a.shape[-1])
    p = {
        "lru_conv_w": lru_conv_w, "lru_conv_b": row3(lru_conv_b),
        "lru_gate_a_w": lru_gate_a_w, "lru_gate_a_b": row3(lru_gate_a_b),
        "lru_gate_x_w": lru_gate_x_w, "lru_gate_x_b": row3(lru_gate_x_b),
        "lru_lambda": row3(lru_lambda), "hgrn_lb_logits": hgrn_lb_logits, "fox_f_bias": fox_f_bias,
        "mix_norm_w": row3(mix_norm_w), "ffn_w_up": ffn_w_up, "ffn_conv_w": ffn_conv_w,
        "ffn_conv_b": row3(ffn_conv_b), "ffn_w_down": ffn_w_down,
    }
    ln1 = row3(ln1_w)
    ln2 = row3(ln2_w)
    w_in_t = jnp.swapaxes(w_in, 1, 2)
    w_f_t = jnp.pad(w_in_t[:, d_main:, :], ((0, 0), (0, LANES - fox_heads), (0, 0)))
    f_bias = jnp.pad(fox_f_bias, ((0, 0), (0, LANES - fox_heads)))

    bm = _pick(n, (512, 256, 128))
    bm_big = _pick(n, (1024, 512, 256, 128))
    bn_in = _pick(d_main, (1024, 512, 256, 128))
    bn_out = _pick(d_model, (1024, 512, 256, 128))
    bn_down = _pick(d_model, (512, 256, 128))
    k_half = d_ff // 2
    assert k_half % LANES == 0

    xf = x.reshape(n, d_model)
    for l in range(depth):
        hn = _rmsnorm(xf, ln1, l, BF16)
        proj = _matmul_stream([hn], w_in_t, w_lead=(l,), k_block=0, k_size=d_model, n_cols=d_main, bn=bn_in,
                              bm=bm_big, w_transposed=True, name="in_proj")
        flog = _matmul([hn], w_f_t, w_lead=(l,), k_block=0, k_size=d_model, n_cols=LANES, bn=LANES, bm=bm_big,
                       w_transposed=True, name="fox_forget_proj")
        cum, cumt = _fox_cum(flog, f_bias[l:l + 1], batch, seq)

        out_lru = _lru_group(proj, p, l, batch, seq)
        out_hg = _hgrn_group(proj, p, l, batch, seq, col0=2 * lru_w // hg_w)
        out_fox = _fox_group(proj, cum, cumt, p, l, batch, seq,
                             qcol0=(2 * lru_w + 4 * hg_w) // HEAD_DIM, gcol0=(lru_w + hg_w) // HEAD_DIM)

        xf = _matmul_stream([out_lru, out_hg, out_fox], w_out, w_lead=(l,), k_block=0, k_size=d_mix,
                            n_cols=d_model, bn=bn_out, bm=bm, res=xf, name="out_proj")

        hn2 = _rmsnorm(xf, ln2, l, BF16)
        hmid = _ffn_up(hn2, p, l, seq)
        for kb in range(2):
            xf = _matmul_stream([hmid], ffn_w_down, w_lead=(l,), k_block=kb, k_size=k_half, n_cols=d_model,
                                bn=bn_down, bm=bm_big, res=xf, name=f"ffn_down_{kb}")

    out = _rmsnorm(xf, final_norm_w.reshape(1, 1, d_model), 0, x.dtype)
    return out.reshape(batch, seq, d_model)
```

```python
import functools

import jax
import jax.numpy as jnp
from jax import lax
from jax.experimental import pallas as pl
from jax.experimental.pallas import tpu as pltpu

F32 = jnp.float32
BF16 = jnp.bfloat16

HEAD_DIM = 128
LRU_C = 8.0
EPS = 1e-6
NEG_BIG = -1e30
LOG2_E = 1.4426950408889634

SUBLANES = 8
LANES = 128
VMEM_CAP_BYTES = 64 * 1024 * 1024
VMEM_MARGIN_BYTES = 6 * 1024 * 1024

NT_DIMS = (((1,), (1,)), ((), ()))
FFN_ROW_CHUNK = 512
FFN_TAIL_CHUNK = 256


def _ffn_row_chunks(bm):
    if bm < FFN_ROW_CHUNK + 2 * FFN_TAIL_CHUNK:
        return (bm,) if bm <= FFN_TAIL_CHUNK else (bm - FFN_TAIL_CHUNK, FFN_TAIL_CHUNK)
    n_big = (bm - 2 * FFN_TAIL_CHUNK) // FFN_ROW_CHUNK
    rest = bm - n_big * FFN_ROW_CHUNK
    return (FFN_ROW_CHUNK,) * n_big + (rest - FFN_TAIL_CHUNK, FFN_TAIL_CHUNK)


def _vmem_limit(estimate_bytes):
    return int(min(max(2 * estimate_bytes, 16 * 1024 * 1024), VMEM_CAP_BYTES - VMEM_MARGIN_BYTES))


def _params(semantics, vmem_estimate):
    return pltpu.CompilerParams(dimension_semantics=semantics, vmem_limit_bytes=_vmem_limit(vmem_estimate))


def _pick(n, candidates):
    for c in candidates:
        if n % c == 0:
            return c
    raise ValueError(f"no tile in {candidates} divides {n}")


def _row_iota(shape):
    return lax.broadcasted_iota(jnp.int32, shape, 0)


def _shift_rows(x, shift, fill):
    rolled = pltpu.roll(x, shift, 0)
    return jnp.where(_row_iota(x.shape) >= shift, rolled, fill)


def _cumsum_rows(x):
    t, w = x.shape
    ngroups = t // SUBLANES
    x3 = x.reshape(ngroups, SUBLANES, w)
    pos = lax.broadcasted_iota(jnp.int32, (1, SUBLANES, 1), 1)
    sh = 1
    while sh < SUBLANES:
        x3 = x3 + jnp.where(pos >= sh, pltpu.roll(x3, sh, 1), 0.0)
        sh *= 2
    groups = [x3[0]]
    for g in range(1, ngroups):
        groups.append(x3[g] + groups[-1][SUBLANES - 1:SUBLANES, :])
    return jnp.concatenate(groups, axis=0)


def _sigmoid(x):
    return 0.5 * jnp.tanh(0.5 * x) + 0.5


def _silu(x):
    h = 0.5 * x
    return h + h * jnp.tanh(h)


def _log_sigmoid(x):
    return jnp.minimum(x, 0.0) - jnp.log1p(jnp.exp(-jnp.abs(x)))


def _softplus(x):
    return jnp.maximum(x, 0.0) + jnp.log1p(jnp.exp(-jnp.abs(x)))


def _head_rms(o, gain):
    var = jnp.mean(o * o, axis=-1, keepdims=True)
    return o * lax.rsqrt(var + EPS) * gain


def _rmsnorm_kernel(x_ref, g_ref, o_ref):
    x = x_ref[...]
    var = jnp.mean(x * x, axis=-1, keepdims=True)
    o_ref[...] = (x * lax.rsqrt(var + EPS) * g_ref[...]).astype(o_ref.dtype)


def _rmsnorm(x, gains, layer, out_dtype):
    n, d = x.shape
    rows = _pick(n, (512, 256, 128, 64, 8))
    est = 2 * rows * d * (4 + jnp.dtype(out_dtype).itemsize) + 2 * rows * d * 4
    return pl.pallas_call(
        _rmsnorm_kernel,
        grid=(n // rows,),
        in_specs=[pl.BlockSpec((rows, d), lambda i: (i, 0)),
                  pl.BlockSpec((None, 1, d), lambda i: (layer, 0, 0))],
        out_specs=pl.BlockSpec((rows, d), lambda i: (i, 0)),
        out_shape=jax.ShapeDtypeStruct((n, d), out_dtype),
        compiler_params=_params(("arbitrary",), est),
        name="rmsnorm",
    )(x, gains)


def _cast_weight(w_ref, wbf_ref, chunk, col0=0):
    cols = w_ref.shape[1]

    def body(c, carry):
        r = pl.multiple_of(c * chunk, chunk)
        wbf_ref[pl.ds(r, chunk), col0:col0 + cols] = w_ref[pl.ds(r, chunk), :].astype(BF16)
        return carry
    lax.fori_loop(0, w_ref.shape[0] // chunk, body, 0)


def _cast_weight_transposed(wt_ref, wbf_ref, chunk):
    for c in range(wt_ref.shape[1] // chunk):
        wbf_ref[c * chunk:(c + 1) * chunk, :] = wt_ref[:, c * chunk:(c + 1) * chunk].T.astype(BF16)


def _mm_kernel(*refs, a_segments, has_res, cast_chunk, w_transposed):
    n_a = len(a_segments)
    a_refs = refs[:n_a]
    w_ref = refs[n_a]
    res_ref = refs[n_a + 1] if has_res else None
    o_ref = refs[n_a + 1 + has_res]
    wbf_ref = refs[n_a + 2 + has_res]

    @pl.when(pl.program_id(1) == 0)
    def _():
        if w_transposed:
            _cast_weight_transposed(w_ref, wbf_ref, cast_chunk)
        else:
            _cast_weight(w_ref, wbf_ref, cast_chunk)

    acc = None
    for a_ref, (off, width) in zip(a_refs, a_segments):
        part = jnp.dot(a_ref[...], wbf_ref[off:off + width, :], preferred_element_type=F32)
        acc = part if acc is None else acc + part
    if has_res:
        acc = acc + res_ref[...]
    o_ref[...] = acc.astype(o_ref.dtype)


def _matmul(a_list, w, *, w_lead, k_block, k_size, n_cols, col_block0=0, bn, bm, res=None,
            out_dtype=F32, w_transposed=False, name):
    n = a_list[0].shape[0]
    segs, off = [], 0
    for a in a_list:
        width = k_size if len(a_list) == 1 else a.shape[1]
        segs.append((off, width))
        off += width
    assert off == k_size
    lead = tuple(w_lead)
    if w_transposed:
        w_spec = pl.BlockSpec((None,) * len(lead) + (bn, k_size), lambda j, i: lead + (j + col_block0, k_block))
    else:
        w_spec = pl.BlockSpec((None,) * len(lead) + (k_size, bn), lambda j, i: lead + (k_block, j + col_block0))
    in_specs = [pl.BlockSpec((bm, width), lambda j, i, kb=(k_block if len(a_list) == 1 else 0): (i, kb))
                for (_, width) in segs]
    in_specs.append(w_spec)
    args = list(a_list) + [w]
    if res is not None:
        in_specs.append(pl.BlockSpec((bm, bn), lambda j, i: (i, j)))
        args.append(res)
    out_bytes = jnp.dtype(out_dtype).itemsize
    est = (2 * k_size * bn * 4 + k_size * bn * 2 + 2 * bm * k_size * 2 + 2 * bm * bn * out_bytes
           + (2 * bm * bn * 4 if res is not None else 0) + bm * bn * 4)
    kern = functools.partial(_mm_kernel, a_segments=tuple(segs), has_res=res is not None,
                             cast_chunk=_pick(k_size, (256, 128) if w_transposed else (256, 128, 64, 8)),
                             w_transposed=w_transposed)
    return pl.pallas_call(
        kern,
        grid=(n_cols // bn, n // bm),
        in_specs=in_specs,
        out_specs=pl.BlockSpec((bm, bn), lambda j, i: (i, j)),
        out_shape=jax.ShapeDtypeStruct((n, n_cols), out_dtype),
        scratch_shapes=[pltpu.VMEM((k_size, bn), BF16)],
        compiler_params=pltpu.CompilerParams(dimension_semantics=("arbitrary", "arbitrary"),
                                             vmem_limit_bytes=min(est + VMEM_MARGIN_BYTES,
                                                                  VMEM_CAP_BYTES - VMEM_MARGIN_BYTES // 2)),
        name=name,
    )(*args)


def _mm_stream_kernel(*refs, a_segments, has_res, w_transposed):
    n_a = len(a_segments)
    a_refs = refs[:n_a]
    w_ref = refs[n_a]
    res_ref = refs[n_a + 1] if has_res else None
    o_ref = refs[n_a + 1 + has_res]
    wbf_refs = refs[n_a + 2 + has_res:n_a + 4 + has_res]
    jj = pl.program_id(0)
    ck = w_ref.shape[1] if w_transposed else w_ref.shape[0]
    r = pl.multiple_of(pl.program_id(1) * ck, ck)

    def stage(dst_ref):
        chunk = w_ref[...].T if w_transposed else w_ref[...]
        dst_ref[pl.ds(r, ck), :] = chunk.astype(BF16)

    def compute(src_ref):
        acc = None
        for a_ref, (off, width) in zip(a_refs, a_segments):
            part = jnp.dot(a_ref[...], src_ref[off:off + width, :], preferred_element_type=F32)
            acc = part if acc is None else acc + part
        if has_res:
            acc = acc + res_ref[...]
        o_ref[...] = acc.astype(o_ref.dtype)

    @pl.when(jj == 0)
    def _():
        stage(wbf_refs[0])

    for parity in range(2):
        @pl.when((jj > 0) & (jj % 2 == parity))
        def _(parity=parity):
            stage(wbf_refs[parity])
            compute(wbf_refs[1 - parity])


def _matmul_stream(a_list, w, *, w_lead, k_block, k_size, n_cols, col_block0=0, bn, bm, res=None,
                   out_dtype=F32, w_transposed=False, name):
    n = a_list[0].shape[0]
    segs, off = [], 0
    for a in a_list:
        width = k_size if len(a_list) == 1 else a.shape[1]
        segs.append((off, width))
        off += width
    assert off == k_size
    ni, nj = n // bm, n_cols // bn
    assert k_size % ni == 0
    ck = k_size // ni
    assert ck % (LANES if w_transposed else 2 * SUBLANES) == 0
    lead = tuple(w_lead)
    row = lambda jj, i: jnp.where(jj == 0, 0, i)
    col = lambda jj: jnp.maximum(jj - 1, 0)
    nxt = lambda jj: jnp.minimum(jj, nj - 1) + col_block0
    if w_transposed:
        w_spec = pl.BlockSpec((None,) * len(lead) + (bn, ck), lambda jj, i: lead + (nxt(jj), k_block * ni + i))
    else:
        w_spec = pl.BlockSpec((None,) * len(lead) + (ck, bn), lambda jj, i: lead + (k_block * ni + i, nxt(jj)))
    in_specs = [pl.BlockSpec((bm, width), lambda jj, i, kb=(k_block if len(a_list) == 1 else 0): (row(jj, i), kb))
                for (_, width) in segs]
    in_specs.append(w_spec)
    args = list(a_list) + [w]
    if res is not None:
        in_specs.append(pl.BlockSpec((bm, bn), lambda jj, i: (row(jj, i), col(jj))))
        args.append(res)
    out_bytes = jnp.dtype(out_dtype).itemsize
    est = (2 * k_size * bn * 2 + 3 * ck * bn * 4 + 2 * bm * k_size * 2 + 2 * bm * bn * out_bytes
           + (2 * bm * bn * 4 if res is not None else 0) + 2 * bm * bn * 4)
    kern = functools.partial(_mm_stream_kernel, a_segments=tuple(segs), has_res=res is not None,
                             w_transposed=w_transposed)
    return pl.pallas_call(
        kern,
        grid=(nj + 1, ni),
        in_specs=in_specs,
        out_specs=pl.BlockSpec((bm, bn), lambda jj, i: (row(jj, i), col(jj))),
        out_shape=jax.ShapeDtypeStruct((n, n_cols), out_dtype),
        scratch_shapes=[pltpu.VMEM((k_size, bn), BF16), pltpu.VMEM((k_size, bn), BF16)],
        compiler_params=pltpu.CompilerParams(dimension_semantics=("arbitrary", "arbitrary"),
                                             vmem_limit_bytes=min(est + VMEM_MARGIN_BYTES,
                                                                  VMEM_CAP_BYTES - VMEM_MARGIN_BYTES // 2)),
        name=name,
    )(*args)


def _lru_kernel(x_ref, y_ref, cw_ref, cb_ref, wa_ref, ba_ref, wx_ref, bx_ref, lam_ref, gain_ref,
                o_ref, xext_ref, hcar_ref):
    t, width = x_ref.shape
    nblk = width // HEAD_DIM

    @pl.when(pl.program_id(1) == 0)
    def _():
        xext_ref[0:SUBLANES, :] = jnp.zeros((SUBLANES, width), F32)
        hcar_ref[...] = jnp.zeros_like(hcar_ref)

    xext_ref[SUBLANES:SUBLANES + t, :] = x_ref[...]
    cw = cw_ref[...]
    kw = cw.shape[0]
    xc = cb_ref[...]
    for s in range(kw):
        xc = xc + cw[kw - 1 - s:kw - s, :] * xext_ref[SUBLANES - s:SUBLANES - s + t, :]
    xext_ref[0:SUBLANES, :] = xext_ref[t:t + SUBLANES, :]

    xcb = xc.astype(BF16)
    ra, rx = [], []
    for n in range(nblk):
        xs = xcb[:, n * HEAD_DIM:(n + 1) * HEAD_DIM]
        ra.append(jnp.dot(xs, wa_ref[n].astype(BF16), preferred_element_type=F32))
        rx.append(jnp.dot(xs, wx_ref[n].astype(BF16), preferred_element_type=F32))
    r = _sigmoid(jnp.concatenate(ra, axis=1) + ba_ref[...])
    gi = _sigmoid(jnp.concatenate(rx, axis=1) + bx_ref[...])
    log_a = r * ((-LRU_C) * _softplus(-lam_ref[...]))
    a = jnp.exp(log_a)
    u = jnp.sqrt(-jnp.tanh(log_a) * (a * a + 1.0)) * (gi * xc)

    ngroups = t // SUBLANES
    u = u.reshape(ngroups, SUBLANES, width)
    a = a.reshape(ngroups, SUBLANES, width)
    pos = lax.broadcasted_iota(jnp.int32, (1, SUBLANES, 1), 1)
    sh = 1
    while sh < SUBLANES:
        inside = pos >= sh
        u = u + a * jnp.where(inside, pltpu.roll(u, sh, 1), 0.0)
        a = a * jnp.where(inside, pltpu.roll(a, sh, 1), 1.0)
        sh *= 2
    h_prev = hcar_ref[0:1, :]
    groups = []
    for g in range(ngroups):
        hg = u[g] + a[g] * h_prev
        groups.append(hg)
        h_prev = hg[SUBLANES - 1:SUBLANES, :]
    h = jnp.concatenate(groups, axis=0)
    hcar_ref[...] = jnp.broadcast_to(h_prev, hcar_ref.shape)

    gate = jax.nn.gelu(y_ref[...])
    gain = gain_ref[...]
    for n in range(nblk):
        sl = slice(n * HEAD_DIM, (n + 1) * HEAD_DIM)
        o_ref[:, sl] = (_head_rms(h[:, sl], gain[:, sl]) * gate[:, sl]).astype(o_ref.dtype)


def _lru_group(proj, p, layer, batch, seq):
    width = p["lru_conv_w"].shape[-1]
    nblk = width // HEAD_DIM
    t = _pick(seq, (256, 128, 64, 8))
    nc = seq // t
    kw = p["lru_conv_w"].shape[1]
    row = lambda b, c: (b * nc + c, 0)
    vec = lambda b, c: (layer, 0, 0)
    est = 2 * 2 * t * width * 4 + 2 * t * width * 2 + 16 * t * width * 4
    return pl.pallas_call(
        _lru_kernel,
        grid=(batch, nc),
        in_specs=[pl.BlockSpec((t, width), row),
                  pl.BlockSpec((t, width), lambda b, c: (b * nc + c, 1)),
                  pl.BlockSpec((None, kw, width), vec),
                  pl.BlockSpec((None, 1, width), vec),
                  pl.BlockSpec((None, nblk, HEAD_DIM, HEAD_DIM), lambda b, c: (layer, 0, 0, 0)),
                  pl.BlockSpec((None, 1, width), vec),
                  pl.BlockSpec((None, nblk, HEAD_DIM, HEAD_DIM), lambda b, c: (layer, 0, 0, 0)),
                  pl.BlockSpec((None, 1, width), vec),
                  pl.BlockSpec((None, 1, width), vec),
                  pl.BlockSpec((None, 1, width), vec)],
        out_specs=pl.BlockSpec((t, width), row),
        out_shape=jax.ShapeDtypeStruct((batch * seq, width), BF16),
        scratch_shapes=[pltpu.VMEM((t + SUBLANES, width), F32), pltpu.VMEM((SUBLANES, width), F32)],
        compiler_params=_params(("arbitrary", "arbitrary"), est),
        name="rg_lru",
    )(proj, proj, p["lru_conv_w"], p["lru_conv_b"], p["lru_gate_a_w"], p["lru_gate_a_b"],
      p["lru_gate_x_w"], p["lru_gate_x_b"], p["lru_lambda"], p["mix_norm_w"])


def _block_mid_rows(x, m, rows):
    t, w = x.shape
    if m >= SUBLANES:
        return jnp.concatenate(
            [jnp.broadcast_to(x[b * 2 * m + m - 1:b * 2 * m + m, :], (2 * m, w)) for b in range(t // (2 * m))], axis=0)
    x3 = x.reshape(t // SUBLANES, SUBLANES, w)

    def pick(i):
        return jnp.broadcast_to(x3[:, i:i + 1, :], x3.shape).reshape(t, w)

    pos = rows & (SUBLANES - 1)
    out = pick(m - 1)
    for b in range(1, SUBLANES // (2 * m)):
        out = jnp.where(pos >= b * 2 * m, pick(b * 2 * m + m - 1), out)
    return out


def _hgrn_kernel(q_ref, f_ref, i_ref, g_ref, lbl_ref, gain_ref, o_ref, state_ref, *, layer):
    t, width = q_ref.shape
    nheads = width // HEAD_DIM

    @pl.when(pl.program_id(1) == 0)
    def _():
        state_ref[...] = jnp.zeros_like(state_ref)

    lbl = lbl_ref[...]
    pe = jnp.exp(lbl - jnp.max(lbl, axis=0, keepdims=True))
    psm = pe / jnp.sum(pe, axis=0, keepdims=True)
    lb = jnp.zeros((1, width), F32)
    for d in range(1, layer + 1):
        lb = lb + psm[d:d + 1, :]

    z = f_ref[...]
    la = jnp.log(lb)
    lbz = jnp.log1p(-lb) + _log_sigmoid(z)
    mx = jnp.maximum(la, lbz)
    log_f = mx + jnp.log(jnp.exp(la - mx) + jnp.exp(lbz - mx))
    k_all = (1.0 - lb) * _sigmoid(-z)
    qz = q_ref[...]
    q_all = _silu(qz)
    v_all = i_ref[...]
    gate = _sigmoid(g_ref[...])
    gain = gain_ref[...]

    rows = _row_iota((t, 1))
    rr = _row_iota((t, t))
    cc = lax.broadcasted_iota(jnp.int32, (t, t), 1)

    cum_all = _cumsum_rows(log_f)

    for h in range(nheads):
        sl = slice(h * HEAD_DIM, (h + 1) * HEAD_DIM)
        q, k, v = q_all[:, sl], k_all[:, sl], v_all[:, sl]
        cum = cum_all[:, sl]
        last = cum[t - 1:t, :]
        st = state_ref[h]
        o = lax.dot_general((q * jnp.exp(cum)).astype(BF16), st.astype(BF16), NT_DIMS,
                            preferred_element_type=F32)

        s_mat = jnp.where(rr == cc, jnp.sum(q * k, axis=-1, keepdims=True), 0.0)
        m = 1
        while 2 * m <= t:
            cref = _block_mid_rows(cum, m, rows)
            e = jnp.exp(-jnp.abs(cum - cref))
            s_l = lax.dot_general((q * e).astype(BF16), (k * e).astype(BF16), NT_DIMS, preferred_element_type=F32)
            keep = ((rr ^ cc) < 2 * m) & ((rr & (2 * m - 1)) >= m) & ((cc & (2 * m - 1)) < m)
            s_mat = s_mat + jnp.where(keep, s_l, 0.0)
            m *= 2
        o = o + jnp.dot(s_mat.astype(BF16), v.astype(BF16), preferred_element_type=F32)

        kd = (k * jnp.exp(last - cum)).astype(BF16)
        state_ref[h] = st * jnp.exp(last) + jnp.dot(v.T.astype(BF16), kd, preferred_element_type=F32)

        o_ref[:, sl] = (_head_rms(o, gain[:, sl]) * gate[:, sl]).astype(o_ref.dtype)


def _hgrn_group(proj, p, layer, batch, seq, col0):
    depth, width = p["hgrn_lb_logits"].shape
    nheads = width // HEAD_DIM
    t = _pick(seq, (128, 64, 32, 16))
    nc = seq // t
    blk = lambda k: pl.BlockSpec((t, width), lambda b, c, k=k: (b * nc + c, col0 + k))
    gcol = p["lru_conv_w"].shape[-1] // width
    est = 2 * 4 * t * width * 4 + 2 * t * width * 2 + nheads * HEAD_DIM * HEAD_DIM * 4 + 24 * t * width * 4
    return pl.pallas_call(
        functools.partial(_hgrn_kernel, layer=layer),
        grid=(batch, nc),
        in_specs=[blk(0), blk(1), blk(2), blk(3),
                  pl.BlockSpec((depth, width), lambda b, c: (0, 0)),
                  pl.BlockSpec((None, 1, width), lambda b, c: (layer, 0, gcol))],
        out_specs=pl.BlockSpec((t, width), lambda b, c: (b * nc + c, 0)),
        out_shape=jax.ShapeDtypeStruct((batch * seq, width), BF16),
        scratch_shapes=[pltpu.VMEM((nheads, HEAD_DIM, HEAD_DIM), F32)],
        compiler_params=_params(("arbitrary", "arbitrary"), est),
        name="hgrn2",
    )(proj, proj, proj, proj, p["hgrn_lb_logits"], p["mix_norm_w"])


def _fox_cum_kernel(f_ref, b_ref, cumt_ref):
    s = f_ref.shape[0]
    cum = _log_sigmoid(f_ref[...] + b_ref[...])
    sh = 1
    while sh < s:
        cum = cum + _shift_rows(cum, sh, 0.0)
        sh *= 2
    cumt_ref[...] = (cum * LOG2_E).T


def _fox_cum(flog, bias_row, batch, seq):
    est = 2 * 2 * seq * LANES * 4 + 8 * seq * LANES * 4
    return pl.pallas_call(
        _fox_cum_kernel,
        grid=(batch,),
        in_specs=[pl.BlockSpec((seq, LANES), lambda b: (b, 0)),
                  pl.BlockSpec((1, LANES), lambda b: (0, 0))],
        out_specs=pl.BlockSpec((None, LANES, seq), lambda b: (b, 0, 0)),
        out_shape=jax.ShapeDtypeStruct((batch, LANES, seq), F32),
        compiler_params=_params(("arbitrary",), est),
        name="fox_cum",
    )(flog, bias_row)


def _fox_kernel(q_ref, k_ref, v_ref, ck_ref, gain_ref, o_ref, kbf_ref, vbf_ref, s_ref, m_ref, acc_ref):
    tq = q_ref.shape[0]
    qi = pl.program_id(2)

    @pl.when(qi == 0)
    def _():
        _cast_weight(k_ref, kbf_ref, tq)

        def body(c, carry):
            r = pl.multiple_of(c * tq, tq)
            vbf_ref[pl.ds(r, tq), 0:HEAD_DIM] = v_ref[pl.ds(r, tq), :].astype(BF16)
            vbf_ref[pl.ds(r, tq), HEAD_DIM:2 * HEAD_DIM] = jnp.ones((tq, HEAD_DIM), BF16)
            return carry
        lax.fori_loop(0, v_ref.shape[0] // tq, body, 0)

    q = (q_ref[...] * (HEAD_DIM ** -0.5 * LOG2_E)).astype(BF16)

    m_ref[...] = jnp.full_like(m_ref, NEG_BIG)

    def lane_tile_max(s):
        part = s[:, 0:LANES]
        for c in range(1, s.shape[1] // LANES):
            part = jnp.maximum(part, s[:, c * LANES:(c + 1) * LANES])
        return part

    def logits(ki, masked):
        r0 = pl.multiple_of(ki * tq, tq)
        s = lax.dot_general(q, kbf_ref[pl.ds(r0, tq), :], NT_DIMS, preferred_element_type=F32)
        s = s - ck_ref[pl.ds(ki, 1), :]
        if masked:
            s = jnp.where(_row_iota((tq, tq)) >= lax.broadcasted_iota(jnp.int32, (tq, tq), 1), s, NEG_BIG)
        s_ref[ki] = s
        m_ref[...] = jnp.maximum(m_ref[...], lane_tile_max(s))

    def pass1(ki, carry):
        logits(ki, False)
        return carry

    lax.fori_loop(0, qi, pass1, 0)
    logits(qi, True)

    m = jnp.max(m_ref[...], axis=-1, keepdims=True)
    acc_ref[...] = jnp.zeros_like(acc_ref)

    def pass2(ki, carry):
        r0 = pl.multiple_of(ki * tq, tq)
        pr = jnp.exp2(s_ref[ki] - m).astype(BF16)
        acc_ref[...] += jnp.dot(pr, vbf_ref[pl.ds(r0, tq), :], preferred_element_type=F32)
        return carry

    lax.fori_loop(0, qi + 1, pass2, 0)

    acc = acc_ref[...]
    o = acc[:, 0:HEAD_DIM] / acc[:, HEAD_DIM:HEAD_DIM + 1]
    o_ref[...] = _head_rms(o, gain_ref[...]).astype(o_ref.dtype)


def _fox_group(proj, cumt, p, layer, batch, seq, qcol0, gcol0):
    nheads = p["fox_f_bias"].shape[-1]
    tq = _pick(seq, (1024, 512, 256, 128))
    nq = seq // tq
    cumt4 = cumt.reshape(batch, LANES, nq, tq)
    est = (2 * tq * HEAD_DIM * 4 + 2 * 2 * seq * HEAD_DIM * 4 + 3 * seq * HEAD_DIM * 2 + 2 * tq * LANES * 4
           + 2 * nq * tq * 4 + 2 * tq * HEAD_DIM * 2 + 3 * tq * LANES * 4 + (nq + 4) * tq * tq * 4)
    return pl.pallas_call(
        _fox_kernel,
        grid=(batch, nheads, nq),
        in_specs=[pl.BlockSpec((tq, HEAD_DIM), lambda b, h, i: (b * nq + i, qcol0 + h)),
                  pl.BlockSpec((seq, HEAD_DIM), lambda b, h, i: (b, qcol0 + nheads + h)),
                  pl.BlockSpec((seq, HEAD_DIM), lambda b, h, i: (b, qcol0 + 2 * nheads + h)),
                  pl.BlockSpec((None, None, nq, tq), lambda b, h, i: (b, h, 0, 0)),
                  pl.BlockSpec((None, 1, HEAD_DIM), lambda b, h, i: (layer, 0, gcol0 + h))],
        out_specs=pl.BlockSpec((tq, HEAD_DIM), lambda b, h, i: (b * nq + i, h)),
        out_shape=jax.ShapeDtypeStruct((batch * seq, nheads * HEAD_DIM), BF16),
        scratch_shapes=[pltpu.VMEM((seq, HEAD_DIM), BF16), pltpu.VMEM((seq, 2 * HEAD_DIM), BF16),
                        pltpu.VMEM((nq, tq, tq), F32), pltpu.VMEM((tq, LANES), F32),
                        pltpu.VMEM((tq, 2 * HEAD_DIM), F32)],
        compiler_params=_params(("arbitrary", "arbitrary", "arbitrary"), est),
        name="fox_attention",
    )(proj, proj, proj, cumt4, p["mix_norm_w"])


def _ffn_up_kernel(a_ref, wg_ref, wv_ref, cwg_ref, cwv_ref, cbg_ref, cbv_ref, o_ref,
                   wa_ref, wb_ref, u_ref, *, tiles_per_seq, row_chunks):
    jj = pl.program_id(0)
    i = pl.program_id(1)
    bm = a_ref.shape[0]
    ck, bn = wg_ref.shape
    r = pl.multiple_of(i * ck, ck)

    def stage(dst_ref):
        dst_ref[pl.ds(r, ck), 0:bn] = wg_ref[...].astype(BF16)
        dst_ref[pl.ds(r, ck), bn:2 * bn] = wv_ref[...].astype(BF16)

    def compute(src_ref):
        cw = jnp.concatenate([cwg_ref[...], cwv_ref[...]], axis=1)
        cb = jnp.concatenate([cbg_ref[...], cbv_ref[...]], axis=1)
        kw = cw.shape[0]
        r0 = 0
        for nrows in row_chunks:
            rows = slice(r0, r0 + nrows)
            u_ref[SUBLANES + r0:SUBLANES + r0 + nrows, :] = jnp.dot(a_ref[rows, :], src_ref[...],
                                                                    preferred_element_type=F32)
            y = cb
            for s in range(kw):
                start = SUBLANES + r0 - s
                y = y + cw[kw - 1 - s:kw - s, :] * u_ref[start:start + nrows, :]
            yg, yv = y[:, 0:bn], y[:, bn:2 * bn]
            o_ref[rows, :] = (_silu(yg) * yv).astype(o_ref.dtype)
            r0 += nrows
        u_ref[0:SUBLANES, :] = u_ref[bm:bm + SUBLANES, :]

    @pl.when(jj == 0)
    def _():
        stage(wa_ref)

    @pl.when((jj > 0) & (i % tiles_per_seq == 0))
    def _():
        u_ref[0:SUBLANES, :] = jnp.zeros((SUBLANES, 2 * bn), F32)

    for parity, (dst_ref, src_ref) in enumerate(((wa_ref, wb_ref), (wb_ref, wa_ref))):
        @pl.when((jj > 0) & (jj % 2 == parity))
        def _(dst_ref=dst_ref, src_ref=src_ref):
            stage(dst_ref)
            compute(src_ref)


def _ffn_up(hn, p, layer, seq):
    n, d = hn.shape
    d_ff = p["ffn_w_down"].shape[1]
    kw = p["ffn_conv_w"].shape[1]
    bn = _pick(d_ff, (256, 128))
    bm = _pick(seq, (2048, 1024, 512, 256))
    nj, ni = d_ff // bn, n // bm
    assert d % ni == 0 and (d // ni) % (2 * SUBLANES) == 0
    ck = d // ni
    est = (2 * bm * d * 2 + 2 * d * 2 * bn * 2 + 2 * 2 * ck * bn * 4 + 2 * bm * bn * 2
           + (bm + SUBLANES) * 2 * bn * 4 + 8 * FFN_ROW_CHUNK * 2 * bn * 4)
    kern = functools.partial(_ffn_up_kernel, tiles_per_seq=seq // bm, row_chunks=_ffn_row_chunks(bm))
    row = lambda jj, i: jnp.where(jj == 0, 0, i)
    col = lambda jj: jnp.maximum(jj - 1, 0)
    nxt = lambda jj: jnp.minimum(jj, nj - 1)
    return pl.pallas_call(
        kern,
        grid=(nj + 1, ni),
        in_specs=[pl.BlockSpec((bm, d), lambda jj, i: (row(jj, i), 0)),
                  pl.BlockSpec((None, ck, bn), lambda jj, i: (layer, i, nxt(jj))),
                  pl.BlockSpec((None, ck, bn), lambda jj, i: (layer, i, nxt(jj) + nj)),
                  pl.BlockSpec((None, kw, bn), lambda jj, i: (layer, 0, col(jj))),
                  pl.BlockSpec((None, kw, bn), lambda jj, i: (layer, 0, col(jj) + nj)),
                  pl.BlockSpec((None, 1, bn), lambda jj, i: (layer, 0, col(jj))),
                  pl.BlockSpec((None, 1, bn), lambda jj, i: (layer, 0, col(jj) + nj))],
        out_specs=pl.BlockSpec((bm, bn), lambda jj, i: (row(jj, i), col(jj))),
        out_shape=jax.ShapeDtypeStruct((n, d_ff), BF16),
        scratch_shapes=[pltpu.VMEM((d, 2 * bn), BF16), pltpu.VMEM((d, 2 * bn), BF16),
                        pltpu.VMEM((bm + SUBLANES, 2 * bn), F32)],
        compiler_params=pltpu.CompilerParams(dimension_semantics=("arbitrary", "arbitrary"),
                                             vmem_limit_bytes=min(est + VMEM_MARGIN_BYTES,
                                                                  VMEM_CAP_BYTES - VMEM_MARGIN_BYTES // 2)),
        name="ffn_up_conv_gate",
    )(hn, p["ffn_w_up"], p["ffn_w_up"], p["ffn_conv_w"], p["ffn_conv_w"], p["ffn_conv_b"], p["ffn_conv_b"])


def kernel(x, ln1_w, w_in, lru_conv_w, lru_conv_b, lru_gate_a_w, lru_gate_a_b, lru_gate_x_w, lru_gate_x_b,
           lru_lambda, hgrn_lb_logits, fox_f_bias, mix_norm_w, w_out, ln2_w, ffn_w_up, ffn_conv_w, ffn_conv_b,
           ffn_w_down, final_norm_w):
    batch, seq, d_model = x.shape
    depth = w_in.shape[0]
    n = batch * seq
    lru_w = lru_conv_w.shape[-1]
    hg_w = hgrn_lb_logits.shape[-1]
    fox_heads = fox_f_bias.shape[-1]
    fox_w = fox_heads * HEAD_DIM
    d_main = 2 * lru_w + 4 * hg_w + 3 * fox_w
    d_mix = lru_w + hg_w + fox_w
    d_ff = ffn_w_down.shape[1]
    assert w_in.shape[-1] == d_main + fox_heads and fox_heads <= LANES
    assert lru_w == hg_w, "column-block addressing of the projection assumes equal LRU / HGRN2 widths"

    row3 = lambda a: a.reshape(a.shape[0], 1, a.shape[-1])
    p = {
        "lru_conv_w": lru_conv_w, "lru_conv_b": row3(lru_conv_b),
        "lru_gate_a_w": lru_gate_a_w, "lru_gate_a_b": row3(lru_gate_a_b),
        "lru_gate_x_w": lru_gate_x_w, "lru_gate_x_b": row3(lru_gate_x_b),
        "lru_lambda": row3(lru_lambda), "hgrn_lb_logits": hgrn_lb_logits, "fox_f_bias": fox_f_bias,
        "mix_norm_w": row3(mix_norm_w), "ffn_w_up": ffn_w_up, "ffn_conv_w": ffn_conv_w,
        "ffn_conv_b": row3(ffn_conv_b), "ffn_w_down": ffn_w_down,
    }
    ln1 = row3(ln1_w)
    ln2 = row3(ln2_w)
    w_in_t = jnp.swapaxes(w_in, 1, 2)
    w_f_t = jnp.pad(w_in_t[:, d_main:, :], ((0, 0), (0, LANES - fox_heads), (0, 0)))
    f_bias = jnp.pad(fox_f_bias, ((0, 0), (0, LANES - fox_heads)))

    bm = _pick(n, (512, 256, 128))
    bm_big = _pick(n, (1024, 512, 256, 128))
    bn_in = _pick(d_main, (1024, 512, 256, 128))
    bn_out = _pick(d_model, (1024, 512, 256, 128))
    bn_down = _pick(d_model, (512, 256, 128))
    k_half = d_ff // 2
    assert k_half % LANES == 0

    xf = x.reshape(n, d_model)
    for l in range(depth):
        hn = _rmsnorm(xf, ln1, l, BF16)
        proj = _matmul_stream([hn], w_in_t, w_lead=(l,), k_block=0, k_size=d_model, n_cols=d_main, bn=bn_in,
                              bm=bm_big, w_transposed=True, name="in_proj")
        flog = _matmul([hn], w_f_t, w_lead=(l,), k_block=0, k_size=d_model, n_cols=LANES, bn=LANES, bm=bm_big,
                       w_transposed=True, name="fox_forget_proj")
        cumt = _fox_cum(flog, f_bias[l:l + 1], batch, seq)

        out_lru = _lru_group(proj, p, l, batch, seq)
        out_hg = _hgrn_group(proj, p, l, batch, seq, col0=2 * lru_w // hg_w)
        out_fox = _fox_group(proj, cumt, p, l, batch, seq,
                             qcol0=(2 * lru_w + 4 * hg_w) // HEAD_DIM, gcol0=(lru_w + hg_w) // HEAD_DIM)

        xf = _matmul_stream([out_lru, out_hg, out_fox], w_out, w_lead=(l,), k_block=0, k_size=d_mix,
                            n_cols=d_model, bn=bn_out, bm=bm, res=xf, name="out_proj")

        hn2 = _rmsnorm(xf, ln2, l, BF16)
        hmid = _ffn_up(hn2, p, l, seq)
        for kb in range(2):
            xf = _matmul_stream([hmid], ffn_w_down, w_lead=(l,), k_block=kb, k_size=k_half, n_cols=d_model,
                                bn=bn_down, bm=bm_big, res=xf, name=f"ffn_down_{kb}")

    out = _rmsnorm(xf, final_norm_w.reshape(1, 1, d_model), 0, x.dtype)
    return out.reshape(batch, seq, d_model)
```

```python
import functools

import jax
import jax.numpy as jnp
from jax import lax
from jax.experimental import pallas as pl
from jax.experimental.pallas import tpu as pltpu

F32 = jnp.float32
BF16 = jnp.bfloat16

HEAD_DIM = 128
LRU_C = 8.0
EPS = 1e-6
NEG_BIG = -1e30
LOG2_E = 1.4426950408889634

SUBLANES = 8
LANES = 128
VMEM_CAP_BYTES = 64 * 1024 * 1024
VMEM_MARGIN_BYTES = 6 * 1024 * 1024

NT_DIMS = (((1,), (1,)), ((), ()))
FFN_ROW_CHUNK = 512
FFN_TAIL_CHUNK = 256


def _ffn_row_chunks(bm):
    if bm < FFN_ROW_CHUNK + 2 * FFN_TAIL_CHUNK:
        return (bm,) if bm <= FFN_TAIL_CHUNK else (bm - FFN_TAIL_CHUNK, FFN_TAIL_CHUNK)
    n_big = (bm - 2 * FFN_TAIL_CHUNK) // FFN_ROW_CHUNK
    rest = bm - n_big * FFN_ROW_CHUNK
    return (FFN_ROW_CHUNK,) * n_big + (rest - FFN_TAIL_CHUNK, FFN_TAIL_CHUNK)


def _vmem_limit(estimate_bytes):
    return int(min(max(2 * estimate_bytes, 16 * 1024 * 1024), VMEM_CAP_BYTES - VMEM_MARGIN_BYTES))


def _params(semantics, vmem_estimate):
    return pltpu.CompilerParams(dimension_semantics=semantics, vmem_limit_bytes=_vmem_limit(vmem_estimate))


def _pick(n, candidates):
    for c in candidates:
        if n % c == 0:
            return c
    raise ValueError(f"no tile in {candidates} divides {n}")


def _row_iota(shape):
    return lax.broadcasted_iota(jnp.int32, shape, 0)


def _shift_rows(x, shift, fill):
    rolled = pltpu.roll(x, shift, 0)
    return jnp.where(_row_iota(x.shape) >= shift, rolled, fill)


def _cumsum_rows(x):
    t, w = x.shape
    ngroups = t // SUBLANES
    x3 = x.reshape(ngroups, SUBLANES, w)
    pos = lax.broadcasted_iota(jnp.int32, (1, SUBLANES, 1), 1)
    sh = 1
    while sh < SUBLANES:
        x3 = x3 + jnp.where(pos >= sh, pltpu.roll(x3, sh, 1), 0.0)
        sh *= 2
    groups = [x3[0]]
    for g in range(1, ngroups):
        groups.append(x3[g] + groups[-1][SUBLANES - 1:SUBLANES, :])
    return jnp.concatenate(groups, axis=0)


def _sigmoid(x):
    return 0.5 * jnp.tanh(0.5 * x) + 0.5


def _silu(x):
    h = 0.5 * x
    return h + h * jnp.tanh(h)


def _log_sigmoid(x):
    return jnp.minimum(x, 0.0) - jnp.log1p(jnp.exp(-jnp.abs(x)))


def _softplus(x):
    return jnp.maximum(x, 0.0) + jnp.log1p(jnp.exp(-jnp.abs(x)))


def _head_rms(o, gain):
    var = jnp.mean(o * o, axis=-1, keepdims=True)
    return o * lax.rsqrt(var + EPS) * gain


def _rmsnorm_kernel(x_ref, g_ref, o_ref):
    x = x_ref[...]
    var = jnp.mean(x * x, axis=-1, keepdims=True)
    o_ref[...] = (x * lax.rsqrt(var + EPS) * g_ref[...]).astype(o_ref.dtype)


def _rmsnorm(x, gains, layer, out_dtype):
    n, d = x.shape
    rows = _pick(n, (512, 256, 128, 64, 8))
    est = 2 * rows * d * (4 + jnp.dtype(out_dtype).itemsize) + 2 * rows * d * 4
    return pl.pallas_call(
        _rmsnorm_kernel,
        grid=(n // rows,),
        in_specs=[pl.BlockSpec((rows, d), lambda i: (i, 0)),
                  pl.BlockSpec((None, 1, d), lambda i: (layer, 0, 0))],
        out_specs=pl.BlockSpec((rows, d), lambda i: (i, 0)),
        out_shape=jax.ShapeDtypeStruct((n, d), out_dtype),
        compiler_params=_params(("arbitrary",), est),
        name="rmsnorm",
    )(x, gains)


def _rmsnorm_proj_kernel(x_ref, g_ref, wt_ref, o_ref, p_ref, wbf_ref):
    @pl.when(pl.program_id(0) == 0)
    def _():
        _cast_weight_transposed(wt_ref, wbf_ref, LANES)

    x = x_ref[...]
    var = jnp.mean(x * x, axis=-1, keepdims=True)
    hn = (x * lax.rsqrt(var + EPS) * g_ref[...]).astype(o_ref.dtype)
    o_ref[...] = hn
    p_ref[...] = jnp.dot(hn, wbf_ref[...], preferred_element_type=F32)


def _rmsnorm_proj(x, gains, layer, w_t, out_dtype):
    n, d = x.shape
    cols = w_t.shape[1]
    rows = _pick(n, (512, 256, 128, 64, 8))
    est = (2 * rows * d * (4 + jnp.dtype(out_dtype).itemsize) + 2 * rows * d * 4 + 2 * cols * d * 4 + cols * d * 2
           + 2 * rows * cols * 4)
    return pl.pallas_call(
        _rmsnorm_proj_kernel,
        grid=(n // rows,),
        in_specs=[pl.BlockSpec((rows, d), lambda i: (i, 0)),
                  pl.BlockSpec((None, 1, d), lambda i: (layer, 0, 0)),
                  pl.BlockSpec((None, cols, d), lambda i: (layer, 0, 0))],
        out_specs=[pl.BlockSpec((rows, d), lambda i: (i, 0)),
                   pl.BlockSpec((rows, cols), lambda i: (i, 0))],
        out_shape=[jax.ShapeDtypeStruct((n, d), out_dtype), jax.ShapeDtypeStruct((n, cols), F32)],
        scratch_shapes=[pltpu.VMEM((d, cols), BF16)],
        compiler_params=_params(("arbitrary",), est),
        name="rmsnorm_forget_proj",
    )(x, gains, w_t)


def _cast_weight(w_ref, wbf_ref, chunk, col0=0):
    cols = w_ref.shape[1]

    def body(c, carry):
        r = pl.multiple_of(c * chunk, chunk)
        wbf_ref[pl.ds(r, chunk), col0:col0 + cols] = w_ref[pl.ds(r, chunk), :].astype(BF16)
        return carry
    lax.fori_loop(0, w_ref.shape[0] // chunk, body, 0)


def _cast_weight_transposed(wt_ref, wbf_ref, chunk):
    for c in range(wt_ref.shape[1] // chunk):
        wbf_ref[c * chunk:(c + 1) * chunk, :] = wt_ref[:, c * chunk:(c + 1) * chunk].T.astype(BF16)


def _mm_stream_kernel(*refs, a_segments, has_res, w_transposed):
    n_a = len(a_segments)
    a_refs = refs[:n_a]
    w_ref = refs[n_a]
    res_ref = refs[n_a + 1] if has_res else None
    o_ref = refs[n_a + 1 + has_res]
    wbf_refs = refs[n_a + 2 + has_res:n_a + 4 + has_res]
    jj = pl.program_id(0)
    ck = w_ref.shape[1] if w_transposed else w_ref.shape[0]
    r = pl.multiple_of(pl.program_id(1) * ck, ck)

    def stage(dst_ref):
        chunk = w_ref[...].T if w_transposed else w_ref[...]
        dst_ref[pl.ds(r, ck), :] = chunk.astype(BF16)

    def compute(src_ref):
        acc = None
        for a_ref, (off, width) in zip(a_refs, a_segments):
            part = jnp.dot(a_ref[...], src_ref[off:off + width, :], preferred_element_type=F32)
            acc = part if acc is None else acc + part
        if has_res:
            acc = acc + res_ref[...]
        o_ref[...] = acc.astype(o_ref.dtype)

    @pl.when(jj == 0)
    def _():
        stage(wbf_refs[0])

    for parity in range(2):
        @pl.when((jj > 0) & (jj % 2 == parity))
        def _(parity=parity):
            stage(wbf_refs[parity])
            compute(wbf_refs[1 - parity])


def _matmul_stream(a_list, w, *, w_lead, k_block, k_size, n_cols, col_block0=0, bn, bm, res=None,
                   out_dtype=F32, w_transposed=False, name):
    n = a_list[0].shape[0]
    segs, off = [], 0
    for a in a_list:
        width = k_size if len(a_list) == 1 else a.shape[1]
        segs.append((off, width))
        off += width
    assert off == k_size
    ni, nj = n // bm, n_cols // bn
    assert k_size % ni == 0
    ck = k_size // ni
    assert ck % (LANES if w_transposed else 2 * SUBLANES) == 0
    lead = tuple(w_lead)
    row = lambda jj, i: jnp.where(jj == 0, 0, i)
    col = lambda jj: jnp.maximum(jj - 1, 0)
    nxt = lambda jj: jnp.minimum(jj, nj - 1) + col_block0
    if w_transposed:
        w_spec = pl.BlockSpec((None,) * len(lead) + (bn, ck), lambda jj, i: lead + (nxt(jj), k_block * ni + i))
    else:
        w_spec = pl.BlockSpec((None,) * len(lead) + (ck, bn), lambda jj, i: lead + (k_block * ni + i, nxt(jj)))
    in_specs = [pl.BlockSpec((bm, width), lambda jj, i, kb=(k_block if len(a_list) == 1 else 0): (row(jj, i), kb))
                for (_, width) in segs]
    in_specs.append(w_spec)
    args = list(a_list) + [w]
    if res is not None:
        in_specs.append(pl.BlockSpec((bm, bn), lambda jj, i: (row(jj, i), col(jj))))
        args.append(res)
    out_bytes = jnp.dtype(out_dtype).itemsize
    est = (2 * k_size * bn * 2 + 3 * ck * bn * 4 + 2 * bm * k_size * 2 + 2 * bm * bn * out_bytes
           + (2 * bm * bn * 4 if res is not None else 0) + 2 * bm * bn * 4)
    kern = functools.partial(_mm_stream_kernel, a_segments=tuple(segs), has_res=res is not None,
                             w_transposed=w_transposed)
    return pl.pallas_call(
        kern,
        grid=(nj + 1, ni),
        in_specs=in_specs,
        out_specs=pl.BlockSpec((bm, bn), lambda jj, i: (row(jj, i), col(jj))),
        out_shape=jax.ShapeDtypeStruct((n, n_cols), out_dtype),
        scratch_shapes=[pltpu.VMEM((k_size, bn), BF16), pltpu.VMEM((k_size, bn), BF16)],
        compiler_params=pltpu.CompilerParams(dimension_semantics=("arbitrary", "arbitrary"),
                                             vmem_limit_bytes=min(est + VMEM_MARGIN_BYTES,
                                                                  VMEM_CAP_BYTES - VMEM_MARGIN_BYTES // 2)),
        name=name,
    )(*args)


def _lru_kernel(x_ref, y_ref, cw_ref, cb_ref, wa_ref, ba_ref, wx_ref, bx_ref, lam_ref, gain_ref,
                o_ref, xext_ref, hcar_ref):
    t, width = x_ref.shape
    nblk = width // HEAD_DIM

    @pl.when(pl.program_id(1) == 0)
    def _():
        xext_ref[0:SUBLANES, :] = jnp.zeros((SUBLANES, width), F32)
        hcar_ref[...] = jnp.zeros_like(hcar_ref)

    xext_ref[SUBLANES:SUBLANES + t, :] = x_ref[...]
    cw = cw_ref[...]
    kw = cw.shape[0]
    xc = cb_ref[...]
    for s in range(kw):
        xc = xc + cw[kw - 1 - s:kw - s, :] * xext_ref[SUBLANES - s:SUBLANES - s + t, :]
    xext_ref[0:SUBLANES, :] = xext_ref[t:t + SUBLANES, :]

    xcb = xc.astype(BF16)
    ra, rx = [], []
    for n in range(nblk):
        xs = xcb[:, n * HEAD_DIM:(n + 1) * HEAD_DIM]
        ra.append(jnp.dot(xs, wa_ref[n].astype(BF16), preferred_element_type=F32))
        rx.append(jnp.dot(xs, wx_ref[n].astype(BF16), preferred_element_type=F32))
    r = _sigmoid(jnp.concatenate(ra, axis=1) + ba_ref[...])
    gi = _sigmoid(jnp.concatenate(rx, axis=1) + bx_ref[...])
    log_a = r * ((-LRU_C) * _softplus(-lam_ref[...]))
    a = jnp.exp(log_a)
    u = jnp.sqrt(-jnp.tanh(log_a) * (a * a + 1.0)) * (gi * xc)

    ngroups = t // SUBLANES
    u = u.reshape(ngroups, SUBLANES, width)
    a = a.reshape(ngroups, SUBLANES, width)
    pos = lax.broadcasted_iota(jnp.int32, (1, SUBLANES, 1), 1)
    sh = 1
    while sh < SUBLANES:
        inside = pos >= sh
        u = u + a * jnp.where(inside, pltpu.roll(u, sh, 1), 0.0)
        a = a * jnp.where(inside, pltpu.roll(a, sh, 1), 1.0)
        sh *= 2
    h_prev = hcar_ref[0:1, :]
    groups = []
    for g in range(ngroups):
        hg = u[g] + a[g] * h_prev
        groups.append(hg)
        h_prev = hg[SUBLANES - 1:SUBLANES, :]
    h = jnp.concatenate(groups, axis=0)
    hcar_ref[...] = jnp.broadcast_to(h_prev, hcar_ref.shape)

    gate = jax.nn.gelu(y_ref[...])
    gain = gain_ref[...]
    for n in range(nblk):
        sl = slice(n * HEAD_DIM, (n + 1) * HEAD_DIM)
        o_ref[:, sl] = (_head_rms(h[:, sl], gain[:, sl]) * gate[:, sl]).astype(o_ref.dtype)


def _lru_group(proj, p, layer, batch, seq):
    width = p["lru_conv_w"].shape[-1]
    nblk = width // HEAD_DIM
    t = _pick(seq, (256, 128, 64, 8))
    nc = seq // t
    kw = p["lru_conv_w"].shape[1]
    row = lambda b, c: (b * nc + c, 0)
    vec = lambda b, c: (layer, 0, 0)
    est = 2 * 2 * t * width * 4 + 2 * t * width * 2 + 16 * t * width * 4
    return pl.pallas_call(
        _lru_kernel,
        grid=(batch, nc),
        in_specs=[pl.BlockSpec((t, width), row),
                  pl.BlockSpec((t, width), lambda b, c: (b * nc + c, 1)),
                  pl.BlockSpec((None, kw, width), vec),
                  pl.BlockSpec((None, 1, width), vec),
                  pl.BlockSpec((None, nblk, HEAD_DIM, HEAD_DIM), lambda b, c: (layer, 0, 0, 0)),
                  pl.BlockSpec((None, 1, width), vec),
                  pl.BlockSpec((None, nblk, HEAD_DIM, HEAD_DIM), lambda b, c: (layer, 0, 0, 0)),
                  pl.BlockSpec((None, 1, width), vec),
                  pl.BlockSpec((None, 1, width), vec),
                  pl.BlockSpec((None, 1, width), vec)],
        out_specs=pl.BlockSpec((t, width), row),
        out_shape=jax.ShapeDtypeStruct((batch * seq, width), BF16),
        scratch_shapes=[pltpu.VMEM((t + SUBLANES, width), F32), pltpu.VMEM((SUBLANES, width), F32)],
        compiler_params=_params(("arbitrary", "arbitrary"), est),
        name="rg_lru",
    )(proj, proj, p["lru_conv_w"], p["lru_conv_b"], p["lru_gate_a_w"], p["lru_gate_a_b"],
      p["lru_gate_x_w"], p["lru_gate_x_b"], p["lru_lambda"], p["mix_norm_w"])


def _block_mid_rows(x, m, rows):
    t, w = x.shape
    if m >= SUBLANES:
        return jnp.concatenate(
            [jnp.broadcast_to(x[b * 2 * m + m - 1:b * 2 * m + m, :], (2 * m, w)) for b in range(t // (2 * m))], axis=0)
    x3 = x.reshape(t // SUBLANES, SUBLANES, w)

    def pick(i):
        return jnp.broadcast_to(x3[:, i:i + 1, :], x3.shape).reshape(t, w)

    pos = rows & (SUBLANES - 1)
    out = pick(m - 1)
    for b in range(1, SUBLANES // (2 * m)):
        out = jnp.where(pos >= b * 2 * m, pick(b * 2 * m + m - 1), out)
    return out


def _hgrn_kernel(q_ref, f_ref, i_ref, g_ref, lbl_ref, gain_ref, o_ref, state_ref, *, layer):
    t, width = q_ref.shape
    nheads = width // HEAD_DIM

    @pl.when(pl.program_id(1) == 0)
    def _():
        state_ref[...] = jnp.zeros_like(state_ref)

    lbl = lbl_ref[...]
    pe = jnp.exp(lbl - jnp.max(lbl, axis=0, keepdims=True))
    psm = pe / jnp.sum(pe, axis=0, keepdims=True)
    lb = jnp.zeros((1, width), F32)
    for d in range(1, layer + 1):
        lb = lb + psm[d:d + 1, :]

    z = f_ref[...]
    la = jnp.log(lb)
    lbz = jnp.log1p(-lb) + _log_sigmoid(z)
    mx = jnp.maximum(la, lbz)
    log_f = mx + jnp.log(jnp.exp(la - mx) + jnp.exp(lbz - mx))
    k_all = (1.0 - lb) * _sigmoid(-z)
    qz = q_ref[...]
    q_all = _silu(qz)
    v_all = i_ref[...]
    gate = _sigmoid(g_ref[...])
    gain = gain_ref[...]

    rows = _row_iota((t, 1))
    rr = _row_iota((t, t))
    cc = lax.broadcasted_iota(jnp.int32, (t, t), 1)

    cum_all = _cumsum_rows(log_f)

    for h in range(nheads):
        sl = slice(h * HEAD_DIM, (h + 1) * HEAD_DIM)
        q, k, v = q_all[:, sl], k_all[:, sl], v_all[:, sl]
        cum = cum_all[:, sl]
        last = cum[t - 1:t, :]
        st = state_ref[h]
        o = lax.dot_general((q * jnp.exp(cum)).astype(BF16), st.astype(BF16), NT_DIMS,
                            preferred_element_type=F32)

        s_mat = jnp.where(rr == cc, jnp.sum(q * k, axis=-1, keepdims=True), 0.0)
        m = 1
        while 2 * m <= t:
            cref = _block_mid_rows(cum, m, rows)
            e = jnp.exp(-jnp.abs(cum - cref))
            s_l = lax.dot_general((q * e).astype(BF16), (k * e).astype(BF16), NT_DIMS, preferred_element_type=F32)
            keep = ((rr ^ cc) < 2 * m) & ((rr & (2 * m - 1)) >= m) & ((cc & (2 * m - 1)) < m)
            s_mat = s_mat + jnp.where(keep, s_l, 0.0)
            m *= 2
        o = o + jnp.dot(s_mat.astype(BF16), v.astype(BF16), preferred_element_type=F32)

        kd = (k * jnp.exp(last - cum)).astype(BF16)
        state_ref[h] = st * jnp.exp(last) + jnp.dot(v.T.astype(BF16), kd, preferred_element_type=F32)

        o_ref[:, sl] = (_head_rms(o, gain[:, sl]) * gate[:, sl]).astype(o_ref.dtype)


def _hgrn_group(proj, p, layer, batch, seq, col0):
    depth, width = p["hgrn_lb_logits"].shape
    nheads = width // HEAD_DIM
    t = _pick(seq, (128, 64, 32, 16))
    nc = seq // t
    blk = lambda k: pl.BlockSpec((t, width), lambda b, c, k=k: (b * nc + c, col0 + k))
    gcol = p["lru_conv_w"].shape[-1] // width
    est = 2 * 4 * t * width * 4 + 2 * t * width * 2 + nheads * HEAD_DIM * HEAD_DIM * 4 + 24 * t * width * 4
    return pl.pallas_call(
        functools.partial(_hgrn_kernel, layer=layer),
        grid=(batch, nc),
        in_specs=[blk(0), blk(1), blk(2), blk(3),
                  pl.BlockSpec((depth, width), lambda b, c: (0, 0)),
                  pl.BlockSpec((None, 1, width), lambda b, c: (layer, 0, gcol))],
        out_specs=pl.BlockSpec((t, width), lambda b, c: (b * nc + c, 0)),
        out_shape=jax.ShapeDtypeStruct((batch * seq, width), BF16),
        scratch_shapes=[pltpu.VMEM((nheads, HEAD_DIM, HEAD_DIM), F32)],
        compiler_params=_params(("arbitrary", "arbitrary"), est),
        name="hgrn2",
    )(proj, proj, proj, proj, p["hgrn_lb_logits"], p["mix_norm_w"])


def _fox_cum_kernel(f_ref, b_ref, cumt_ref):
    s = f_ref.shape[0]
    cum = _log_sigmoid(f_ref[...] + b_ref[...])
    sh = 1
    while sh < s:
        cum = cum + _shift_rows(cum, sh, 0.0)
        sh *= 2
    cumt_ref[...] = (cum * LOG2_E).T


def _fox_cum(flog, bias_row, batch, seq):
    est = 2 * 2 * seq * LANES * 4 + 8 * seq * LANES * 4
    return pl.pallas_call(
        _fox_cum_kernel,
        grid=(batch,),
        in_specs=[pl.BlockSpec((seq, LANES), lambda b: (b, 0)),
                  pl.BlockSpec((1, LANES), lambda b: (0, 0))],
        out_specs=pl.BlockSpec((None, LANES, seq), lambda b: (b, 0, 0)),
        out_shape=jax.ShapeDtypeStruct((batch, LANES, seq), F32),
        compiler_params=_params(("arbitrary",), est),
        name="fox_cum",
    )(flog, bias_row)


def _fox_kernel(q_ref, k_ref, v_ref, ck_ref, gain_ref, o_ref, kbf_ref, vbf_ref, s_ref, m_ref, acc_ref):
    tq = q_ref.shape[0]
    qi = pl.program_id(2)

    @pl.when(qi == 0)
    def _():
        _cast_weight(k_ref, kbf_ref, tq)

        def body(c, carry):
            r = pl.multiple_of(c * tq, tq)
            vbf_ref[pl.ds(r, tq), 0:HEAD_DIM] = v_ref[pl.ds(r, tq), :].astype(BF16)
            vbf_ref[pl.ds(r, tq), HEAD_DIM:2 * HEAD_DIM] = jnp.ones((tq, HEAD_DIM), BF16)
            return carry
        lax.fori_loop(0, v_ref.shape[0] // tq, body, 0)

    q = (q_ref[...] * (HEAD_DIM ** -0.5 * LOG2_E)).astype(BF16)

    m_ref[...] = jnp.full_like(m_ref, NEG_BIG)

    def lane_tile_max(s):
        part = s[:, 0:LANES]
        for c in range(1, s.shape[1] // LANES):
            part = jnp.maximum(part, s[:, c * LANES:(c + 1) * LANES])
        return part

    def logits(ki, masked):
        r0 = pl.multiple_of(ki * tq, tq)
        s = lax.dot_general(q, kbf_ref[pl.ds(r0, tq), :], NT_DIMS, preferred_element_type=F32)
        s = s - ck_ref[pl.ds(ki, 1), :]
        if masked:
            s = jnp.where(_row_iota((tq, tq)) >= lax.broadcasted_iota(jnp.int32, (tq, tq), 1), s, NEG_BIG)
        s_ref[ki] = s
        m_ref[...] = jnp.maximum(m_ref[...], lane_tile_max(s))

    def pass1(ki, carry):
        logits(ki, False)
        return carry

    lax.fori_loop(0, qi, pass1, 0)
    logits(qi, True)

    m = jnp.max(m_ref[...], axis=-1, keepdims=True)
    acc_ref[...] = jnp.zeros_like(acc_ref)

    def pass2(ki, carry):
        r0 = pl.multiple_of(ki * tq, tq)
        pr = jnp.exp2(s_ref[ki] - m).astype(BF16)
        acc_ref[...] += jnp.dot(pr, vbf_ref[pl.ds(r0, tq), :], preferred_element_type=F32)
        return carry

    lax.fori_loop(0, qi + 1, pass2, 0)

    acc = acc_ref[...]
    o = acc[:, 0:HEAD_DIM] / acc[:, HEAD_DIM:HEAD_DIM + 1]
    o_ref[...] = _head_rms(o, gain_ref[...]).astype(o_ref.dtype)


def _fox_group(proj, cumt, p, layer, batch, seq, qcol0, gcol0):
    nheads = p["fox_f_bias"].shape[-1]
    tq = _pick(seq, (1024, 512, 256, 128))
    nq = seq // tq
    cumt4 = cumt.reshape(batch, LANES, nq, tq)
    est = (2 * tq * HEAD_DIM * 4 + 2 * 2 * seq * HEAD_DIM * 4 + 3 * seq * HEAD_DIM * 2 + 2 * tq * LANES * 4
           + 2 * nq * tq * 4 + 2 * tq * HEAD_DIM * 2 + 3 * tq * LANES * 4 + (nq + 4) * tq * tq * 4)
    return pl.pallas_call(
        _fox_kernel,
        grid=(batch, nheads, nq),
        in_specs=[pl.BlockSpec((tq, HEAD_DIM), lambda b, h, i: (b * nq + i, qcol0 + h)),
                  pl.BlockSpec((seq, HEAD_DIM), lambda b, h, i: (b, qcol0 + nheads + h)),
                  pl.BlockSpec((seq, HEAD_DIM), lambda b, h, i: (b, qcol0 + 2 * nheads + h)),
                  pl.BlockSpec((None, None, nq, tq), lambda b, h, i: (b, h, 0, 0)),
                  pl.BlockSpec((None, 1, HEAD_DIM), lambda b, h, i: (layer, 0, gcol0 + h))],
        out_specs=pl.BlockSpec((tq, HEAD_DIM), lambda b, h, i: (b * nq + i, h)),
        out_shape=jax.ShapeDtypeStruct((batch * seq, nheads * HEAD_DIM), BF16),
        scratch_shapes=[pltpu.VMEM((seq, HEAD_DIM), BF16), pltpu.VMEM((seq, 2 * HEAD_DIM), BF16),
                        pltpu.VMEM((nq, tq, tq), F32), pltpu.VMEM((tq, LANES), F32),
                        pltpu.VMEM((tq, 2 * HEAD_DIM), F32)],
        compiler_params=_params(("arbitrary", "arbitrary", "arbitrary"), est),
        name="fox_attention",
    )(proj, proj, proj, cumt4, p["mix_norm_w"])


def _ffn_up_kernel(a_ref, wg_ref, wv_ref, cwg_ref, cwv_ref, cbg_ref, cbv_ref, o_ref,
                   wa_ref, wb_ref, u_ref, *, tiles_per_seq, row_chunks):
    jj = pl.program_id(0)
    i = pl.program_id(1)
    bm = a_ref.shape[0]
    ck, bn = wg_ref.shape
    r = pl.multiple_of(i * ck, ck)

    def stage(dst_ref):
        dst_ref[pl.ds(r, ck), 0:bn] = wg_ref[...].astype(BF16)
        dst_ref[pl.ds(r, ck), bn:2 * bn] = wv_ref[...].astype(BF16)

    def compute(src_ref):
        cw = jnp.concatenate([cwg_ref[...], cwv_ref[...]], axis=1)
        cb = jnp.concatenate([cbg_ref[...], cbv_ref[...]], axis=1)
        kw = cw.shape[0]
        r0 = 0
        for nrows in row_chunks:
            rows = slice(r0, r0 + nrows)
            u_ref[SUBLANES + r0:SUBLANES + r0 + nrows, :] = jnp.dot(a_ref[rows, :], src_ref[...],
                                                                    preferred_element_type=F32)
            y = cb
            for s in range(kw):
                start = SUBLANES + r0 - s
                y = y + cw[kw - 1 - s:kw - s, :] * u_ref[start:start + nrows, :]
            yg, yv = y[:, 0:bn], y[:, bn:2 * bn]
            o_ref[rows, :] = (_silu(yg) * yv).astype(o_ref.dtype)
            r0 += nrows
        u_ref[0:SUBLANES, :] = u_ref[bm:bm + SUBLANES, :]

    @pl.when(jj == 0)
    def _():
        stage(wa_ref)

    @pl.when((jj > 0) & (i % tiles_per_seq == 0))
    def _():
        u_ref[0:SUBLANES, :] = jnp.zeros((SUBLANES, 2 * bn), F32)

    for parity, (dst_ref, src_ref) in enumerate(((wa_ref, wb_ref), (wb_ref, wa_ref))):
        @pl.when((jj > 0) & (jj % 2 == parity))
        def _(dst_ref=dst_ref, src_ref=src_ref):
            stage(dst_ref)
            compute(src_ref)


def _ffn_up(hn, p, layer, seq):
    n, d = hn.shape
    d_ff = p["ffn_w_down"].shape[1]
    kw = p["ffn_conv_w"].shape[1]
    bn = _pick(d_ff, (256, 128))
    bm = _pick(seq, (2048, 1024, 512, 256))
    nj, ni = d_ff // bn, n // bm
    assert d % ni == 0 and (d // ni) % (2 * SUBLANES) == 0
    ck = d // ni
    est = (2 * bm * d * 2 + 2 * d * 2 * bn * 2 + 2 * 2 * ck * bn * 4 + 2 * bm * bn * 2
           + (bm + SUBLANES) * 2 * bn * 4 + 8 * FFN_ROW_CHUNK * 2 * bn * 4)
    kern = functools.partial(_ffn_up_kernel, tiles_per_seq=seq // bm, row_chunks=_ffn_row_chunks(bm))
    row = lambda jj, i: jnp.where(jj == 0, 0, i)
    col = lambda jj: jnp.maximum(jj - 1, 0)
    nxt = lambda jj: jnp.minimum(jj, nj - 1)
    return pl.pallas_call(
        kern,
        grid=(nj + 1, ni),
        in_specs=[pl.BlockSpec((bm, d), lambda jj, i: (row(jj, i), 0)),
                  pl.BlockSpec((None, ck, bn), lambda jj, i: (layer, i, nxt(jj))),
                  pl.BlockSpec((None, ck, bn), lambda jj, i: (layer, i, nxt(jj) + nj)),
                  pl.BlockSpec((None, kw, bn), lambda jj, i: (layer, 0, col(jj))),
                  pl.BlockSpec((None, kw, bn), lambda jj, i: (layer, 0, col(jj) + nj)),
                  pl.BlockSpec((None, 1, bn), lambda jj, i: (layer, 0, col(jj))),
                  pl.BlockSpec((None, 1, bn), lambda jj, i: (layer, 0, col(jj) + nj))],
        out_specs=pl.BlockSpec((bm, bn), lambda jj, i: (row(jj, i), col(jj))),
        out_shape=jax.ShapeDtypeStruct((n, d_ff), BF16),
        scratch_shapes=[pltpu.VMEM((d, 2 * bn), BF16), pltpu.VMEM((d, 2 * bn), BF16),
                        pltpu.VMEM((bm + SUBLANES, 2 * bn), F32)],
        compiler_params=pltpu.CompilerParams(dimension_semantics=("arbitrary", "arbitrary"),
                                             vmem_limit_bytes=min(est + VMEM_MARGIN_BYTES,
                                                                  VMEM_CAP_BYTES - VMEM_MARGIN_BYTES // 2)),
        name="ffn_up_conv_gate",
    )(hn, p["ffn_w_up"], p["ffn_w_up"], p["ffn_conv_w"], p["ffn_conv_w"], p["ffn_conv_b"], p["ffn_conv_b"])


def kernel(x, ln1_w, w_in, lru_conv_w, lru_conv_b, lru_gate_a_w, lru_gate_a_b, lru_gate_x_w, lru_gate_x_b,
           lru_lambda, hgrn_lb_logits, fox_f_bias, mix_norm_w, w_out, ln2_w, ffn_w_up, ffn_conv_w, ffn_conv_b,
           ffn_w_down, final_norm_w):
    batch, seq, d_model = x.shape
    depth = w_in.shape[0]
    n = batch * seq
    lru_w = lru_conv_w.shape[-1]
    hg_w = hgrn_lb_logits.shape[-1]
    fox_heads = fox_f_bias.shape[-1]
    fox_w = fox_heads * HEAD_DIM
    d_main = 2 * lru_w + 4 * hg_w + 3 * fox_w
    d_mix = lru_w + hg_w + fox_w
    d_ff = ffn_w_down.shape[1]
    assert w_in.shape[-1] == d_main + fox_heads and fox_heads <= LANES
    assert lru_w == hg_w, "column-block addressing of the projection assumes equal LRU / HGRN2 widths"

    row3 = lambda a: a.reshape(a.shape[0], 1, a.shape[-1])
    p = {
        "lru_conv_w": lru_conv_w, "lru_conv_b": row3(lru_conv_b),
        "lru_gate_a_w": lru_gate_a_w, "lru_gate_a_b": row3(lru_gate_a_b),
        "lru_gate_x_w": lru_gate_x_w, "lru_gate_x_b": row3(lru_gate_x_b),
        "lru_lambda": row3(lru_lambda), "hgrn_lb_logits": hgrn_lb_logits, "fox_f_bias": fox_f_bias,
        "mix_norm_w": row3(mix_norm_w), "ffn_w_up": ffn_w_up, "ffn_conv_w": ffn_conv_w,
        "ffn_conv_b": row3(ffn_conv_b), "ffn_w_down": ffn_w_down,
    }
    ln1 = row3(ln1_w)
    ln2 = row3(ln2_w)
    w_in_t = jnp.swapaxes(w_in, 1, 2)
    w_f_t = jnp.pad(w_in_t[:, d_main:, :], ((0, 0), (0, LANES - fox_heads), (0, 0)))
    f_bias = jnp.pad(fox_f_bias, ((0, 0), (0, LANES - fox_heads)))

    bm = _pick(n, (512, 256, 128))
    bm_big = _pick(n, (1024, 512, 256, 128))
    bn_in = _pick(d_main, (1024, 512, 256, 128))
    bn_out = _pick(d_model, (1024, 512, 256, 128))
    bn_down = _pick(d_model, (512, 256, 128))
    k_half = d_ff // 2
    assert k_half % LANES == 0

    xf = x.reshape(n, d_model)
    for l in range(depth):
        hn, flog = _rmsnorm_proj(xf, ln1, l, w_f_t, BF16)
        proj = _matmul_stream([hn], w_in_t, w_lead=(l,), k_block=0, k_size=d_model, n_cols=d_main, bn=bn_in,
                              bm=bm_big, w_transposed=True, name="in_proj")
        cumt = _fox_cum(flog, f_bias[l:l + 1], batch, seq)

        out_lru = _lru_group(proj, p, l, batch, seq)
        out_hg = _hgrn_group(proj, p, l, batch, seq, col0=2 * lru_w // hg_w)
        out_fox = _fox_group(proj, cumt, p, l, batch, seq,
                             qcol0=(2 * lru_w + 4 * hg_w) // HEAD_DIM, gcol0=(lru_w + hg_w) // HEAD_DIM)

        xf = _matmul_stream([out_lru, out_hg, out_fox], w_out, w_lead=(l,), k_block=0, k_size=d_mix,
                            n_cols=d_model, bn=bn_out, bm=bm, res=xf, name="out_proj")

        hn2 = _rmsnorm(xf, ln2, l, BF16)
        hmid = _ffn_up(hn2, p, l, seq)
        for kb in range(2):
            xf = _matmul_stream([hmid], ffn_w_down, w_lead=(l,), k_block=kb, k_size=k_half, n_cols=d_model,
                                bn=bn_down, bm=bm_big, res=xf, name=f"ffn_down_{kb}")

    out = _rmsnorm(xf, final_norm_w.reshape(1, 1, d_model), 0, x.dtype)
    return out.reshape(batch, seq, d_model)
```

```python
import functools

import jax
import jax.numpy as jnp
from jax import lax
from jax.experimental import pallas as pl
from jax.experimental.pallas import tpu as pltpu

F32 = jnp.float32
BF16 = jnp.bfloat16

HEAD_DIM = 128
LRU_C = 8.0
EPS = 1e-6
NEG_BIG = -1e30
LOG2_E = 1.4426950408889634

SUBLANES = 8
LANES = 128
VMEM_CAP_BYTES = 64 * 1024 * 1024
VMEM_MARGIN_BYTES = 6 * 1024 * 1024

NT_DIMS = (((1,), (1,)), ((), ()))
FFN_ROW_CHUNK = 512
FFN_TAIL_CHUNK = 256


def _ffn_row_chunks(bm):
    if bm < FFN_ROW_CHUNK + 2 * FFN_TAIL_CHUNK:
        return (bm,) if bm <= FFN_TAIL_CHUNK else (bm - FFN_TAIL_CHUNK, FFN_TAIL_CHUNK)
    n_big = (bm - 2 * FFN_TAIL_CHUNK) // FFN_ROW_CHUNK
    rest = bm - n_big * FFN_ROW_CHUNK
    return (FFN_ROW_CHUNK,) * n_big + (rest - FFN_TAIL_CHUNK, FFN_TAIL_CHUNK)


def _vmem_limit(estimate_bytes):
    return int(min(max(2 * estimate_bytes, 16 * 1024 * 1024), VMEM_CAP_BYTES - VMEM_MARGIN_BYTES))


def _params(semantics, vmem_estimate):
    return pltpu.CompilerParams(dimension_semantics=semantics, vmem_limit_bytes=_vmem_limit(vmem_estimate))


def _pick(n, candidates):
    for c in candidates:
        if n % c == 0:
            return c
    raise ValueError(f"no tile in {candidates} divides {n}")


def _row_iota(shape):
    return lax.broadcasted_iota(jnp.int32, shape, 0)


def _shift_rows(x, shift, fill):
    rolled = pltpu.roll(x, shift, 0)
    return jnp.where(_row_iota(x.shape) >= shift, rolled, fill)


def _cumsum_rows(x):
    t, w = x.shape
    ngroups = t // SUBLANES
    x3 = x.reshape(ngroups, SUBLANES, w)
    pos = lax.broadcasted_iota(jnp.int32, (1, SUBLANES, 1), 1)
    sh = 1
    while sh < SUBLANES:
        x3 = x3 + jnp.where(pos >= sh, pltpu.roll(x3, sh, 1), 0.0)
        sh *= 2
    groups = [x3[0]]
    for g in range(1, ngroups):
        groups.append(x3[g] + groups[-1][SUBLANES - 1:SUBLANES, :])
    return jnp.concatenate(groups, axis=0)


def _sigmoid(x):
    return 0.5 * jnp.tanh(0.5 * x) + 0.5


def _silu(x):
    h = 0.5 * x
    return h + h * jnp.tanh(h)


def _log_sigmoid(x):
    return jnp.minimum(x, 0.0) - jnp.log1p(jnp.exp(-jnp.abs(x)))


def _softplus(x):
    return jnp.maximum(x, 0.0) + jnp.log1p(jnp.exp(-jnp.abs(x)))


def _head_rms(o, gain):
    var = jnp.mean(o * o, axis=-1, keepdims=True)
    return o * lax.rsqrt(var + EPS) * gain


def _rmsnorm_kernel(x_ref, g_ref, o_ref):
    x = x_ref[...]
    var = jnp.mean(x * x, axis=-1, keepdims=True)
    o_ref[...] = (x * lax.rsqrt(var + EPS) * g_ref[...]).astype(o_ref.dtype)


def _rmsnorm(x, gains, layer, out_dtype):
    n, d = x.shape
    rows = _pick(n, (512, 256, 128, 64, 8))
    est = 2 * rows * d * (4 + jnp.dtype(out_dtype).itemsize) + 2 * rows * d * 4
    return pl.pallas_call(
        _rmsnorm_kernel,
        grid=(n // rows,),
        in_specs=[pl.BlockSpec((rows, d), lambda i: (i, 0)),
                  pl.BlockSpec((None, 1, d), lambda i: (layer, 0, 0))],
        out_specs=pl.BlockSpec((rows, d), lambda i: (i, 0)),
        out_shape=jax.ShapeDtypeStruct((n, d), out_dtype),
        compiler_params=_params(("arbitrary",), est),
        name="rmsnorm",
    )(x, gains)


def _rmsnorm_proj_kernel(x_ref, g_ref, wt_ref, o_ref, p_ref, wbf_ref):
    @pl.when(pl.program_id(0) == 0)
    def _():
        _cast_weight_transposed(wt_ref, wbf_ref, LANES)

    x = x_ref[...]
    var = jnp.mean(x * x, axis=-1, keepdims=True)
    hn = (x * lax.rsqrt(var + EPS) * g_ref[...]).astype(o_ref.dtype)
    o_ref[...] = hn
    p_ref[...] = jnp.dot(hn, wbf_ref[...], preferred_element_type=F32)


def _rmsnorm_proj(x, gains, layer, w_t, out_dtype):
    n, d = x.shape
    cols = w_t.shape[1]
    rows = _pick(n, (512, 256, 128, 64, 8))
    est = (2 * rows * d * (4 + jnp.dtype(out_dtype).itemsize) + 2 * rows * d * 4 + 2 * cols * d * 4 + cols * d * 2
           + 2 * rows * cols * 4)
    return pl.pallas_call(
        _rmsnorm_proj_kernel,
        grid=(n // rows,),
        in_specs=[pl.BlockSpec((rows, d), lambda i: (i, 0)),
                  pl.BlockSpec((None, 1, d), lambda i: (layer, 0, 0)),
                  pl.BlockSpec((None, cols, d), lambda i: (layer, 0, 0))],
        out_specs=[pl.BlockSpec((rows, d), lambda i: (i, 0)),
                   pl.BlockSpec((rows, cols), lambda i: (i, 0))],
        out_shape=[jax.ShapeDtypeStruct((n, d), out_dtype), jax.ShapeDtypeStruct((n, cols), F32)],
        scratch_shapes=[pltpu.VMEM((d, cols), BF16)],
        compiler_params=_params(("arbitrary",), est),
        name="rmsnorm_forget_proj",
    )(x, gains, w_t)


def _cast_weight(w_ref, wbf_ref, chunk, col0=0):
    cols = w_ref.shape[1]

    def body(c, carry):
        r = pl.multiple_of(c * chunk, chunk)
        wbf_ref[pl.ds(r, chunk), col0:col0 + cols] = w_ref[pl.ds(r, chunk), :].astype(BF16)
        return carry
    lax.fori_loop(0, w_ref.shape[0] // chunk, body, 0)


def _cast_weight_transposed(wt_ref, wbf_ref, chunk):
    for c in range(wt_ref.shape[1] // chunk):
        wbf_ref[c * chunk:(c + 1) * chunk, :] = wt_ref[:, c * chunk:(c + 1) * chunk].T.astype(BF16)


def _mm_stream_kernel(*refs, a_segments, has_res, w_transposed):
    n_a = len(a_segments)
    a_refs = refs[:n_a]
    w_ref = refs[n_a]
    res_ref = refs[n_a + 1] if has_res else None
    o_ref = refs[n_a + 1 + has_res]
    wbf_refs = refs[n_a + 2 + has_res:n_a + 4 + has_res]
    jj = pl.program_id(0)
    ck = w_ref.shape[1] if w_transposed else w_ref.shape[0]
    r = pl.multiple_of(pl.program_id(1) * ck, ck)

    def stage(dst_ref):
        chunk = w_ref[...].T if w_transposed else w_ref[...]
        dst_ref[pl.ds(r, ck), :] = chunk.astype(BF16)

    def compute(src_ref):
        acc = None
        for a_ref, (off, width) in zip(a_refs, a_segments):
            part = jnp.dot(a_ref[...], src_ref[off:off + width, :], preferred_element_type=F32)
            acc = part if acc is None else acc + part
        if has_res:
            acc = acc + res_ref[...]
        o_ref[...] = acc.astype(o_ref.dtype)

    @pl.when(jj == 0)
    def _():
        stage(wbf_refs[0])

    for parity in range(2):
        @pl.when((jj > 0) & (jj % 2 == parity))
        def _(parity=parity):
            stage(wbf_refs[parity])
            compute(wbf_refs[1 - parity])


def _matmul_stream(a_list, w, *, w_lead, k_block, k_size, n_cols, col_block0=0, bn, bm, res=None,
                   out_dtype=F32, w_transposed=False, name):
    n = a_list[0].shape[0]
    segs, off = [], 0
    for a in a_list:
        width = k_size if len(a_list) == 1 else a.shape[1]
        segs.append((off, width))
        off += width
    assert off == k_size
    ni, nj = n // bm, n_cols // bn
    assert k_size % ni == 0
    ck = k_size // ni
    assert ck % (LANES if w_transposed else 2 * SUBLANES) == 0
    lead = tuple(w_lead)
    row = lambda jj, i: jnp.where(jj == 0, 0, i)
    col = lambda jj: jnp.maximum(jj - 1, 0)
    nxt = lambda jj: jnp.minimum(jj, nj - 1) + col_block0
    if w_transposed:
        w_spec = pl.BlockSpec((None,) * len(lead) + (bn, ck), lambda jj, i: lead + (nxt(jj), k_block * ni + i))
    else:
        w_spec = pl.BlockSpec((None,) * len(lead) + (ck, bn), lambda jj, i: lead + (k_block * ni + i, nxt(jj)))
    in_specs = [pl.BlockSpec((bm, width), lambda jj, i, kb=(k_block if len(a_list) == 1 else 0): (row(jj, i), kb))
                for (_, width) in segs]
    in_specs.append(w_spec)
    args = list(a_list) + [w]
    if res is not None:
        in_specs.append(pl.BlockSpec((bm, bn), lambda jj, i: (row(jj, i), col(jj))))
        args.append(res)
    out_bytes = jnp.dtype(out_dtype).itemsize
    est = (2 * k_size * bn * 2 + 3 * ck * bn * 4 + 2 * bm * k_size * 2 + 2 * bm * bn * out_bytes
           + (2 * bm * bn * 4 if res is not None else 0) + 2 * bm * bn * 4)
    kern = functools.partial(_mm_stream_kernel, a_segments=tuple(segs), has_res=res is not None,
                             w_transposed=w_transposed)
    return pl.pallas_call(
        kern,
        grid=(nj + 1, ni),
        in_specs=in_specs,
        out_specs=pl.BlockSpec((bm, bn), lambda jj, i: (row(jj, i), col(jj))),
        out_shape=jax.ShapeDtypeStruct((n, n_cols), out_dtype),
        scratch_shapes=[pltpu.VMEM((k_size, bn), BF16), pltpu.VMEM((k_size, bn), BF16)],
        compiler_params=pltpu.CompilerParams(dimension_semantics=("arbitrary", "arbitrary"),
                                             vmem_limit_bytes=min(est + VMEM_MARGIN_BYTES,
                                                                  VMEM_CAP_BYTES - VMEM_MARGIN_BYTES // 2)),
        name=name,
    )(*args)


def _lru_kernel(x_ref, y_ref, cw_ref, cb_ref, wa_ref, ba_ref, wx_ref, bx_ref, lam_ref, gain_ref,
                o_ref, xext_ref, hcar_ref):
    t, width = x_ref.shape
    nblk = width // HEAD_DIM

    @pl.when(pl.program_id(1) == 0)
    def _():
        xext_ref[0:SUBLANES, :] = jnp.zeros((SUBLANES, width), F32)
        hcar_ref[...] = jnp.zeros_like(hcar_ref)

    xext_ref[SUBLANES:SUBLANES + t, :] = x_ref[...]
    cw = cw_ref[...]
    kw = cw.shape[0]
    xc = cb_ref[...]
    for s in range(kw):
        xc = xc + cw[kw - 1 - s:kw - s, :] * xext_ref[SUBLANES - s:SUBLANES - s + t, :]
    xext_ref[0:SUBLANES, :] = xext_ref[t:t + SUBLANES, :]

    xcb = xc.astype(BF16)
    ra, rx = [], []
    for n in range(nblk):
        xs = xcb[:, n * HEAD_DIM:(n + 1) * HEAD_DIM]
        ra.append(jnp.dot(xs, wa_ref[n].astype(BF16), preferred_element_type=F32))
        rx.append(jnp.dot(xs, wx_ref[n].astype(BF16), preferred_element_type=F32))
    r = _sigmoid(jnp.concatenate(ra, axis=1) + ba_ref[...])
    gi = _sigmoid(jnp.concatenate(rx, axis=1) + bx_ref[...])
    log_a = r * ((-LRU_C) * _softplus(-lam_ref[...]))
    a = jnp.exp(log_a)
    u = jnp.sqrt(-jnp.tanh(log_a) * (a * a + 1.0)) * (gi * xc)

    ngroups = t // SUBLANES
    u = u.reshape(ngroups, SUBLANES, width)
    a = a.reshape(ngroups, SUBLANES, width)
    pos = lax.broadcasted_iota(jnp.int32, (1, SUBLANES, 1), 1)
    sh = 1
    while sh < SUBLANES:
        inside = pos >= sh
        u = u + a * jnp.where(inside, pltpu.roll(u, sh, 1), 0.0)
        a = a * jnp.where(inside, pltpu.roll(a, sh, 1), 1.0)
        sh *= 2
    h_prev = hcar_ref[0:1, :]
    groups = []
    for g in range(ngroups):
        hg = u[g] + a[g] * h_prev
        groups.append(hg)
        h_prev = hg[SUBLANES - 1:SUBLANES, :]
    h = jnp.concatenate(groups, axis=0)
    hcar_ref[...] = jnp.broadcast_to(h_prev, hcar_ref.shape)

    gate = jax.nn.gelu(y_ref[...])
    gain = gain_ref[...]
    for n in range(nblk):
        sl = slice(n * HEAD_DIM, (n + 1) * HEAD_DIM)
        o_ref[:, sl] = (_head_rms(h[:, sl], gain[:, sl]) * gate[:, sl]).astype(o_ref.dtype)


def _lru_group(proj, p, layer, batch, seq):
    width = p["lru_conv_w"].shape[-1]
    nblk = width // HEAD_DIM
    t = _pick(seq, (512, 256, 128, 64, 8))
    nc = seq // t
    kw = p["lru_conv_w"].shape[1]
    row = lambda b, c: (b * nc + c, 0)
    vec = lambda b, c: (layer, 0, 0)
    est = 2 * 2 * t * width * 4 + 2 * t * width * 2 + 16 * t * width * 4
    return pl.pallas_call(
        _lru_kernel,
        grid=(batch, nc),
        in_specs=[pl.BlockSpec((t, width), row),
                  pl.BlockSpec((t, width), lambda b, c: (b * nc + c, 1)),
                  pl.BlockSpec((None, kw, width), vec),
                  pl.BlockSpec((None, 1, width), vec),
                  pl.BlockSpec((None, nblk, HEAD_DIM, HEAD_DIM), lambda b, c: (layer, 0, 0, 0)),
                  pl.BlockSpec((None, 1, width), vec),
                  pl.BlockSpec((None, nblk, HEAD_DIM, HEAD_DIM), lambda b, c: (layer, 0, 0, 0)),
                  pl.BlockSpec((None, 1, width), vec),
                  pl.BlockSpec((None, 1, width), vec),
                  pl.BlockSpec((None, 1, width), vec)],
        out_specs=pl.BlockSpec((t, width), row),
        out_shape=jax.ShapeDtypeStruct((batch * seq, width), BF16),
        scratch_shapes=[pltpu.VMEM((t + SUBLANES, width), F32), pltpu.VMEM((SUBLANES, width), F32)],
        compiler_params=_params(("arbitrary", "arbitrary"), est),
        name="rg_lru",
    )(proj, proj, p["lru_conv_w"], p["lru_conv_b"], p["lru_gate_a_w"], p["lru_gate_a_b"],
      p["lru_gate_x_w"], p["lru_gate_x_b"], p["lru_lambda"], p["mix_norm_w"])


def _block_mid_rows(x, m, rows):
    t, w = x.shape
    if m >= SUBLANES:
        return jnp.concatenate(
            [jnp.broadcast_to(x[b * 2 * m + m - 1:b * 2 * m + m, :], (2 * m, w)) for b in range(t // (2 * m))], axis=0)
    x3 = x.reshape(t // SUBLANES, SUBLANES, w)

    def pick(i):
        return jnp.broadcast_to(x3[:, i:i + 1, :], x3.shape).reshape(t, w)

    pos = rows & (SUBLANES - 1)
    out = pick(m - 1)
    for b in range(1, SUBLANES // (2 * m)):
        out = jnp.where(pos >= b * 2 * m, pick(b * 2 * m + m - 1), out)
    return out


def _hgrn_kernel(q_ref, f_ref, i_ref, g_ref, lbl_ref, gain_ref, o_ref, state_ref, *, layer):
    t, width = q_ref.shape
    nheads = width // HEAD_DIM

    @pl.when(pl.program_id(1) == 0)
    def _():
        state_ref[...] = jnp.zeros_like(state_ref)

    lbl = lbl_ref[...]
    pe = jnp.exp(lbl - jnp.max(lbl, axis=0, keepdims=True))
    psm = pe / jnp.sum(pe, axis=0, keepdims=True)
    lb = jnp.zeros((1, width), F32)
    for d in range(1, layer + 1):
        lb = lb + psm[d:d + 1, :]

    z = f_ref[...]
    la = jnp.log(lb)
    lbz = jnp.log1p(-lb) + _log_sigmoid(z)
    mx = jnp.maximum(la, lbz)
    log_f = mx + jnp.log(jnp.exp(la - mx) + jnp.exp(lbz - mx))
    k_all = (1.0 - lb) * _sigmoid(-z)
    qz = q_ref[...]
    q_all = _silu(qz)
    v_all = i_ref[...]
    gate = _sigmoid(g_ref[...])
    gain = gain_ref[...]

    rows = _row_iota((t, 1))
    rr = _row_iota((t, t))
    cc = lax.broadcasted_iota(jnp.int32, (t, t), 1)

    cum_all = _cumsum_rows(log_f)

    for h in range(nheads):
        sl = slice(h * HEAD_DIM, (h + 1) * HEAD_DIM)
        q, k, v = q_all[:, sl], k_all[:, sl], v_all[:, sl]
        cum = cum_all[:, sl]
        last = cum[t - 1:t, :]
        st = state_ref[h]
        o = lax.dot_general((q * jnp.exp(cum)).astype(BF16), st.astype(BF16), NT_DIMS,
                            preferred_element_type=F32)

        s_mat = jnp.where(rr == cc, jnp.sum(q * k, axis=-1, keepdims=True), 0.0)
        m = 1
        while 2 * m <= t:
            cref = _block_mid_rows(cum, m, rows)
            e = jnp.exp(-jnp.abs(cum - cref))
            s_l = lax.dot_general((q * e).astype(BF16), (k * e).astype(BF16), NT_DIMS, preferred_element_type=F32)
            keep = ((rr ^ cc) < 2 * m) & ((rr & (2 * m - 1)) >= m) & ((cc & (2 * m - 1)) < m)
            s_mat = s_mat + jnp.where(keep, s_l, 0.0)
            m *= 2
        o = o + jnp.dot(s_mat.astype(BF16), v.astype(BF16), preferred_element_type=F32)

        kd = (k * jnp.exp(last - cum)).astype(BF16)
        state_ref[h] = st * jnp.exp(last) + jnp.dot(v.T.astype(BF16), kd, preferred_element_type=F32)

        o_ref[:, sl] = (_head_rms(o, gain[:, sl]) * gate[:, sl]).astype(o_ref.dtype)


def _hgrn_group(proj, p, layer, batch, seq, col0):
    depth, width = p["hgrn_lb_logits"].shape
    nheads = width // HEAD_DIM
    t = _pick(seq, (256, 128, 64, 32, 16))
    nc = seq // t
    blk = lambda k: pl.BlockSpec((t, width), lambda b, c, k=k: (b * nc + c, col0 + k))
    gcol = p["lru_conv_w"].shape[-1] // width
    est = 2 * 4 * t * width * 4 + 2 * t * width * 2 + nheads * HEAD_DIM * HEAD_DIM * 4 + 24 * t * width * 4
    return pl.pallas_call(
        functools.partial(_hgrn_kernel, layer=layer),
        grid=(batch, nc),
        in_specs=[blk(0), blk(1), blk(2), blk(3),
                  pl.BlockSpec((depth, width), lambda b, c: (0, 0)),
                  pl.BlockSpec((None, 1, width), lambda b, c: (layer, 0, gcol))],
        out_specs=pl.BlockSpec((t, width), lambda b, c: (b * nc + c, 0)),
        out_shape=jax.ShapeDtypeStruct((batch * seq, width), BF16),
        scratch_shapes=[pltpu.VMEM((nheads, HEAD_DIM, HEAD_DIM), F32)],
        compiler_params=_params(("arbitrary", "arbitrary"), est),
        name="hgrn2",
    )(proj, proj, proj, proj, p["hgrn_lb_logits"], p["mix_norm_w"])


def _fox_cum_kernel(f_ref, b_ref, cumt_ref):
    s = f_ref.shape[0]
    cum = _log_sigmoid(f_ref[...] + b_ref[...])
    sh = 1
    while sh < s:
        cum = cum + _shift_rows(cum, sh, 0.0)
        sh *= 2
    cumt_ref[...] = (cum * LOG2_E).T


def _fox_cum(flog, bias_row, batch, seq):
    est = 2 * 2 * seq * LANES * 4 + 8 * seq * LANES * 4
    return pl.pallas_call(
        _fox_cum_kernel,
        grid=(batch,),
        in_specs=[pl.BlockSpec((seq, LANES), lambda b: (b, 0)),
                  pl.BlockSpec((1, LANES), lambda b: (0, 0))],
        out_specs=pl.BlockSpec((None, LANES, seq), lambda b: (b, 0, 0)),
        out_shape=jax.ShapeDtypeStruct((batch, LANES, seq), F32),
        compiler_params=_params(("arbitrary",), est),
        name="fox_cum",
    )(flog, bias_row)


def _fox_kernel(q_ref, k_ref, v_ref, ck_ref, gain_ref, o_ref, kbf_ref, vbf_ref, s_ref, m_ref, acc_ref):
    tq = q_ref.shape[0]
    qi = pl.program_id(2)

    @pl.when(qi == 0)
    def _():
        _cast_weight(k_ref, kbf_ref, tq)

        def body(c, carry):
            r = pl.multiple_of(c * tq, tq)
            vbf_ref[pl.ds(r, tq), 0:HEAD_DIM] = v_ref[pl.ds(r, tq), :].astype(BF16)
            vbf_ref[pl.ds(r, tq), HEAD_DIM:2 * HEAD_DIM] = jnp.ones((tq, HEAD_DIM), BF16)
            return carry
        lax.fori_loop(0, v_ref.shape[0] // tq, body, 0)

    q = (q_ref[...] * (HEAD_DIM ** -0.5 * LOG2_E)).astype(BF16)

    m_ref[...] = jnp.full_like(m_ref, NEG_BIG)

    def lane_tile_max(s):
        part = s[:, 0:LANES]
        for c in range(1, s.shape[1] // LANES):
            part = jnp.maximum(part, s[:, c * LANES:(c + 1) * LANES])
        return part

    def logits(ki, masked):
        r0 = pl.multiple_of(ki * tq, tq)
        s = lax.dot_general(q, kbf_ref[pl.ds(r0, tq), :], NT_DIMS, preferred_element_type=F32)
        s = s - ck_ref[pl.ds(ki, 1), :]
        if masked:
            s = jnp.where(_row_iota((tq, tq)) >= lax.broadcasted_iota(jnp.int32, (tq, tq), 1), s, NEG_BIG)
        s_ref[ki] = s
        m_ref[...] = jnp.maximum(m_ref[...], lane_tile_max(s))

    def pass1(ki, carry):
        logits(ki, False)
        return carry

    lax.fori_loop(0, qi, pass1, 0)
    logits(qi, True)

    m = jnp.max(m_ref[...], axis=-1, keepdims=True)
    acc_ref[...] = jnp.zeros_like(acc_ref)

    def pass2(ki, carry):
        r0 = pl.multiple_of(ki * tq, tq)
        pr = jnp.exp2(s_ref[ki] - m).astype(BF16)
        acc_ref[...] += jnp.dot(pr, vbf_ref[pl.ds(r0, tq), :], preferred_element_type=F32)
        return carry

    lax.fori_loop(0, qi + 1, pass2, 0)

    acc = acc_ref[...]
    o = acc[:, 0:HEAD_DIM] / acc[:, HEAD_DIM:HEAD_DIM + 1]
    o_ref[...] = _head_rms(o, gain_ref[...]).astype(o_ref.dtype)


def _fox_group(proj, cumt, p, layer, batch, seq, qcol0, gcol0):
    nheads = p["fox_f_bias"].shape[-1]
    tq = _pick(seq, (1024, 512, 256, 128))
    nq = seq // tq
    cumt4 = cumt.reshape(batch, LANES, nq, tq)
    est = (2 * tq * HEAD_DIM * 4 + 2 * 2 * seq * HEAD_DIM * 4 + 3 * seq * HEAD_DIM * 2 + 2 * tq * LANES * 4
           + 2 * nq * tq * 4 + 2 * tq * HEAD_DIM * 2 + 3 * tq * LANES * 4 + (nq + 4) * tq * tq * 4)
    return pl.pallas_call(
        _fox_kernel,
        grid=(batch, nheads, nq),
        in_specs=[pl.BlockSpec((tq, HEAD_DIM), lambda b, h, i: (b * nq + i, qcol0 + h)),
                  pl.BlockSpec((seq, HEAD_DIM), lambda b, h, i: (b, qcol0 + nheads + h)),
                  pl.BlockSpec((seq, HEAD_DIM), lambda b, h, i: (b, qcol0 + 2 * nheads + h)),
                  pl.BlockSpec((None, None, nq, tq), lambda b, h, i: (b, h, 0, 0)),
                  pl.BlockSpec((None, 1, HEAD_DIM), lambda b, h, i: (layer, 0, gcol0 + h))],
        out_specs=pl.BlockSpec((tq, HEAD_DIM), lambda b, h, i: (b * nq + i, h)),
        out_shape=jax.ShapeDtypeStruct((batch * seq, nheads * HEAD_DIM), BF16),
        scratch_shapes=[pltpu.VMEM((seq, HEAD_DIM), BF16), pltpu.VMEM((seq, 2 * HEAD_DIM), BF16),
                        pltpu.VMEM((nq, tq, tq), F32), pltpu.VMEM((tq, LANES), F32),
                        pltpu.VMEM((tq, 2 * HEAD_DIM), F32)],
        compiler_params=_params(("arbitrary", "arbitrary", "arbitrary"), est),
        name="fox_attention",
    )(proj, proj, proj, cumt4, p["mix_norm_w"])


def _ffn_up_kernel(a_ref, wg_ref, wv_ref, cwg_ref, cwv_ref, cbg_ref, cbv_ref, o_ref,
                   wa_ref, wb_ref, u_ref, *, tiles_per_seq, row_chunks):
    jj = pl.program_id(0)
    i = pl.program_id(1)
    bm = a_ref.shape[0]
    ck, bn = wg_ref.shape
    r = pl.multiple_of(i * ck, ck)

    def stage(dst_ref):
        dst_ref[pl.ds(r, ck), 0:bn] = wg_ref[...].astype(BF16)
        dst_ref[pl.ds(r, ck), bn:2 * bn] = wv_ref[...].astype(BF16)

    def compute(src_ref):
        cw = jnp.concatenate([cwg_ref[...], cwv_ref[...]], axis=1)
        cb = jnp.concatenate([cbg_ref[...], cbv_ref[...]], axis=1)
        kw = cw.shape[0]
        r0 = 0
        for nrows in row_chunks:
            rows = slice(r0, r0 + nrows)
            u_ref[SUBLANES + r0:SUBLANES + r0 + nrows, :] = jnp.dot(a_ref[rows, :], src_ref[...],
                                                                    preferred_element_type=F32)
            y = cb
            for s in range(kw):
                start = SUBLANES + r0 - s
                y = y + cw[kw - 1 - s:kw - s, :] * u_ref[start:start + nrows, :]
            yg, yv = y[:, 0:bn], y[:, bn:2 * bn]
            o_ref[rows, :] = (_silu(yg) * yv).astype(o_ref.dtype)
            r0 += nrows
        u_ref[0:SUBLANES, :] = u_ref[bm:bm + SUBLANES, :]

    @pl.when(jj == 0)
    def _():
        stage(wa_ref)

    @pl.when((jj > 0) & (i % tiles_per_seq == 0))
    def _():
        u_ref[0:SUBLANES, :] = jnp.zeros((SUBLANES, 2 * bn), F32)

    for parity, (dst_ref, src_ref) in enumerate(((wa_ref, wb_ref), (wb_ref, wa_ref))):
        @pl.when((jj > 0) & (jj % 2 == parity))
        def _(dst_ref=dst_ref, src_ref=src_ref):
            stage(dst_ref)
            compute(src_ref)


def _ffn_up(hn, p, layer, seq):
    n, d = hn.shape
    d_ff = p["ffn_w_down"].shape[1]
    kw = p["ffn_conv_w"].shape[1]
    bn = _pick(d_ff, (256, 128))
    bm = _pick(seq, (2048, 1024, 512, 256))
    nj, ni = d_ff // bn, n // bm
    assert d % ni == 0 and (d // ni) % (2 * SUBLANES) == 0
    ck = d // ni
    est = (2 * bm * d * 2 + 2 * d * 2 * bn * 2 + 2 * 2 * ck * bn * 4 + 2 * bm * bn * 2
           + (bm + SUBLANES) * 2 * bn * 4 + 8 * FFN_ROW_CHUNK * 2 * bn * 4)
    kern = functools.partial(_ffn_up_kernel, tiles_per_seq=seq // bm, row_chunks=_ffn_row_chunks(bm))
    row = lambda jj, i: jnp.where(jj == 0, 0, i)
    col = lambda jj: jnp.maximum(jj - 1, 0)
    nxt = lambda jj: jnp.minimum(jj, nj - 1)
    return pl.pallas_call(
        kern,
        grid=(nj + 1, ni),
        in_specs=[pl.BlockSpec((bm, d), lambda jj, i: (row(jj, i), 0)),
                  pl.BlockSpec((None, ck, bn), lambda jj, i: (layer, i, nxt(jj))),
                  pl.BlockSpec((None, ck, bn), lambda jj, i: (layer, i, nxt(jj) + nj)),
                  pl.BlockSpec((None, kw, bn), lambda jj, i: (layer, 0, col(jj))),
                  pl.BlockSpec((None, kw, bn), lambda jj, i: (layer, 0, col(jj) + nj)),
                  pl.BlockSpec((None, 1, bn), lambda jj, i: (layer, 0, col(jj))),
                  pl.BlockSpec((None, 1, bn), lambda jj, i: (layer, 0, col(jj) + nj))],
        out_specs=pl.BlockSpec((bm, bn), lambda jj, i: (row(jj, i), col(jj))),
        out_shape=jax.ShapeDtypeStruct((n, d_ff), BF16),
        scratch_shapes=[pltpu.VMEM((d, 2 * bn), BF16), pltpu.VMEM((d, 2 * bn), BF16),
                        pltpu.VMEM((bm + SUBLANES, 2 * bn), F32)],
        compiler_params=pltpu.CompilerParams(dimension_semantics=("arbitrary", "arbitrary"),
                                             vmem_limit_bytes=min(est + VMEM_MARGIN_BYTES,
                                                                  VMEM_CAP_BYTES - VMEM_MARGIN_BYTES // 2)),
        name="ffn_up_conv_gate",
    )(hn, p["ffn_w_up"], p["ffn_w_up"], p["ffn_conv_w"], p["ffn_conv_w"], p["ffn_conv_b"], p["ffn_conv_b"])


def kernel(x, ln1_w, w_in, lru_conv_w, lru_conv_b, lru_gate_a_w, lru_gate_a_b, lru_gate_x_w, lru_gate_x_b,
           lru_lambda, hgrn_lb_logits, fox_f_bias, mix_norm_w, w_out, ln2_w, ffn_w_up, ffn_conv_w, ffn_conv_b,
           ffn_w_down, final_norm_w):
    batch, seq, d_model = x.shape
    depth = w_in.shape[0]
    n = batch * seq
    lru_w = lru_conv_w.shape[-1]
    hg_w = hgrn_lb_logits.shape[-1]
    fox_heads = fox_f_bias.shape[-1]
    fox_w = fox_heads * HEAD_DIM
    d_main = 2 * lru_w + 4 * hg_w + 3 * fox_w
    d_mix = lru_w + hg_w + fox_w
    d_ff = ffn_w_down.shape[1]
    assert w_in.shape[-1] == d_main + fox_heads and fox_heads <= LANES
    assert lru_w == hg_w, "column-block addressing of the projection assumes equal LRU / HGRN2 widths"

    row3 = lambda a: a.reshape(a.shape[0], 1, a.shape[-1])
    p = {
        "lru_conv_w": lru_conv_w, "lru_conv_b": row3(lru_conv_b),
        "lru_gate_a_w": lru_gate_a_w, "lru_gate_a_b": row3(lru_gate_a_b),
        "lru_gate_x_w": lru_gate_x_w, "lru_gate_x_b": row3(lru_gate_x_b),
        "lru_lambda": row3(lru_lambda), "hgrn_lb_logits": hgrn_lb_logits, "fox_f_bias": fox_f_bias,
        "mix_norm_w": row3(mix_norm_w), "ffn_w_up": ffn_w_up, "ffn_conv_w": ffn_conv_w,
        "ffn_conv_b": row3(ffn_conv_b), "ffn_w_down": ffn_w_down,
    }
    ln1 = row3(ln1_w)
    ln2 = row3(ln2_w)
    w_in_t = jnp.swapaxes(w_in, 1, 2)
    w_f_t = jnp.pad(w_in_t[:, d_main:, :], ((0, 0), (0, LANES - fox_heads), (0, 0)))
    f_bias = jnp.pad(fox_f_bias, ((0, 0), (0, LANES - fox_heads)))

    bm = _pick(n, (512, 256, 128))
    bm_big = _pick(n, (1024, 512, 256, 128))
    bn_in = _pick(d_main, (1024, 512, 256, 128))
    bn_out = _pick(d_model, (1024, 512, 256, 128))
    bn_down = _pick(d_model, (512, 256, 128))
    k_half = d_ff // 2
    assert k_half % LANES == 0

    xf = x.reshape(n, d_model)
    for l in range(depth):
        hn, flog = _rmsnorm_proj(xf, ln1, l, w_f_t, BF16)
        proj = _matmul_stream([hn], w_in_t, w_lead=(l,), k_block=0, k_size=d_model, n_cols=d_main, bn=bn_in,
                              bm=bm_big, w_transposed=True, name="in_proj")
        cumt = _fox_cum(flog, f_bias[l:l + 1], batch, seq)

        out_lru = _lru_group(proj, p, l, batch, seq)
        out_hg = _hgrn_group(proj, p, l, batch, seq, col0=2 * lru_w // hg_w)
        out_fox = _fox_group(proj, cumt, p, l, batch, seq,
                             qcol0=(2 * lru_w + 4 * hg_w) // HEAD_DIM, gcol0=(lru_w + hg_w) // HEAD_DIM)

        xf = _matmul_stream([out_lru, out_hg, out_fox], w_out, w_lead=(l,), k_block=0, k_size=d_mix,
                            n_cols=d_model, bn=bn_out, bm=bm, res=xf, name="out_proj")

        hn2 = _rmsnorm(xf, ln2, l, BF16)
        hmid = _ffn_up(hn2, p, l, seq)
        for kb in range(2):
            xf = _matmul_stream([hmid], ffn_w_down, w_lead=(l,), k_block=kb, k_size=k_half, n_cols=d_model,
                                bn=bn_down, bm=bm_big, res=xf, name=f"ffn_down_{kb}")

    out = _rmsnorm(xf, final_norm_w.reshape(1, 1, d_model), 0, x.dtype)
    return out.reshape(batch, seq, d_model)
```
